```python
import jax, jax.numpy as jnp
from jax import lax
import numpy as np

D_MODEL = 1024
BATCH = 8
SEQ = 2048
DEPTH = 4

CHUNK = 128
A_WIDTH = D_MODEL
A_HEADS = 8
A_HEAD_DIM = A_WIDTH // A_HEADS
B_WIDTH = D_MODEL
B_GROUPS = 16
B_GROUP_DIM = B_WIDTH // B_GROUPS
CONV_WIDTH = 3
D_FF = ((8 * D_MODEL // 3 + 127) // 128) * 128
N_IN = 2 * A_WIDTH + 3 * B_WIDTH + 2 * D_MODEL
EPS = 1e-6

kernel_name = "macaron_gmlp_shortconv_gated_hybrid"


def rmsnorm(x, g):
    x32 = x.astype(jnp.float32)
    y = x32 * lax.rsqrt(jnp.mean(x32 * x32, axis=-1, keepdims=True) + EPS)
    return y.astype(x.dtype) * g


def swiglu(h, w_in, w_out):
    gate, up = jnp.split(h @ w_in, 2, axis=-1)
    return (jax.nn.silu(gate) * up) @ w_out


def spatial_gating(u, v, norm_g, w_s, b_s):
    bsz, s, _ = v.shape
    v = rmsnorm(v, norm_g)
    v = v.reshape(bsz, s // CHUNK, CHUNK, A_HEADS, A_HEAD_DIM)
    v = jnp.einsum('hqp,bcphd->bcqhd', w_s, v) + b_s.T[None, None, :, :, None]
    return u * v.reshape(bsz, s, A_WIDTH)


def short_conv(t, w):
    s = t.shape[1]
    pad = CONV_WIDTH // 2
    tp = jnp.pad(t, ((0, 0), (pad, CONV_WIDTH - 1 - pad), (0, 0)))
    out = w[0] * tp[:, 0:s]
    for k in range(1, CONV_WIDTH):
        out = out + w[k] * tp[:, k:k + s]
    return out


def hybrid_mixer(h, w_in, sgu_norm, sgu_w, sgu_b, conv_w, w_proj_a, w_proj_b, w_out):
    z = h @ w_in
    za, zb, zg = jnp.split(z, [2 * A_WIDTH, 2 * A_WIDTH + 3 * B_WIDTH], axis=-1)
    u, v = jnp.split(jax.nn.gelu(za), 2, axis=-1)
    y_a = spatial_gating(u, v, sgu_norm, sgu_w, sgu_b) @ w_proj_a
    gate_b, gate_c, xb = jnp.split(zb, 3, axis=-1)
    y_b = (gate_b * short_conv(gate_c * xb, conv_w)) @ w_proj_b
    g_a, g_b = jnp.split(zg, 2, axis=-1)
    merged = jax.nn.sigmoid(g_a) * y_a + jax.nn.sigmoid(g_b) * y_b
    return merged @ w_out


def setup_inputs(seed: int = 0) -> dict:
    key = jax.random.key(seed)
    ks = jax.random.split(key, 20)
    f32 = jnp.float32

    def nrm(k, shape, scale):
        return jax.random.normal(k, shape, f32) * scale

    def gain(k, shape):
        return 1.0 + 0.05 * jax.random.normal(k, shape, f32)

    return {
        "x": nrm(ks[0], (BATCH, SEQ, D_MODEL), 1.0),
        "ffn1_norm": gain(ks[1], (DEPTH, D_MODEL)),
        "ffn1_w_in": nrm(ks[2], (DEPTH, D_MODEL, 2 * D_FF), D_MODEL ** -0.5),
        "ffn1_w_out": nrm(ks[3], (DEPTH, D_FF, D_MODEL), D_FF ** -0.5),
        "mix_norm": gain(ks[4], (DEPTH, D_MODEL)),
        "w_in": nrm(ks[5], (DEPTH, D_MODEL, N_IN), D_MODEL ** -0.5),
        "sgu_norm": gain(ks[6], (DEPTH, A_WIDTH)),
        "sgu_w": nrm(ks[7], (DEPTH, A_HEADS, CHUNK, CHUNK), CHUNK ** -0.5),
        "sgu_b": gain(ks[8], (DEPTH, A_HEADS, CHUNK)),
        "conv_w": nrm(ks[9], (DEPTH, CONV_WIDTH, B_WIDTH), CONV_WIDTH ** -0.5),
        "w_proj_a": nrm(ks[10], (DEPTH, A_WIDTH, D_MODEL), A_WIDTH ** -0.5),
        "w_proj_b": nrm(ks[11], (DEPTH, B_WIDTH, D_MODEL), B_WIDTH ** -0.5),
        "w_out": nrm(ks[12], (DEPTH, D_MODEL, D_MODEL), D_MODEL ** -0.5),
        "ffn2_norm": gain(ks[13], (DEPTH, D_MODEL)),
        "ffn2_w_in": nrm(ks[14], (DEPTH, D_MODEL, 2 * D_FF), D_MODEL ** -0.5),
        "ffn2_w_out": nrm(ks[15], (DEPTH, D_FF, D_MODEL), D_FF ** -0.5),
        "final_norm": gain(ks[16], (D_MODEL,)),
    }


def reference(x, ffn1_norm, ffn1_w_in, ffn1_w_out, mix_norm, w_in, sgu_norm, sgu_w,
              sgu_b, conv_w, w_proj_a, w_proj_b, w_out, ffn2_norm, ffn2_w_in,
              ffn2_w_out, final_norm):
    for l in range(DEPTH):
        x = x + 0.5 * swiglu(rmsnorm(x, ffn1_norm[l]), ffn1_w_in[l], ffn1_w_out[l])
        x = x + hybrid_mixer(rmsnorm(x, mix_norm[l]), w_in[l], sgu_norm[l], sgu_w[l],
                             sgu_b[l], conv_w[l], w_proj_a[l], w_proj_b[l], w_out[l])
        x = x + 0.5 * swiglu(rmsnorm(x, ffn2_norm[l]), ffn2_w_in[l], ffn2_w_out[l])
    return rmsnorm(x, final_norm)
```

```python
import functools

import numpy as np
import jax
import jax.numpy as jnp
from jax import lax
from jax.experimental import pallas as pl
from jax.experimental.pallas import tpu as pltpu

F32 = jnp.float32
BF16 = jnp.bfloat16
EPS = 1e-6

D_MODEL = 1024
SEQ = 2048
CHUNK = 128
HEADS = 8
HEAD_DIM = 128
D_FF = 2816
N_IN = 7 * D_MODEL

TOKENS_PER_TILE = SEQ
FFN_COLS = 256
FFN_ROWS = 512
MIX_COLS = 256
N_MIX_BLOCKS = D_MODEL // MIX_COLS
N_CHUNKS = SEQ // CHUNK
HEADS_PER_BLOCK = MIX_COLS // HEAD_DIM
VMEM_LIMIT_BYTES = 60 * 1024 * 1024


def _rmsnorm(x, g):
    ms = jnp.mean(x * x, axis=-1, keepdims=True)
    return (x * lax.rsqrt(ms + EPS)) * g


def _gelu_tanh(x):
    c = np.float32(np.sqrt(2.0 / np.pi))
    return x * (0.5 * (1.0 + jnp.tanh(c * (x + 0.044715 * (x * x * x)))))


def _sigmoid(x):
    return 1.0 / (1.0 + jnp.exp(-x))


def _dot(a, b):
    return jnp.dot(a, b, preferred_element_type=F32)


def _ffn_kernel(x_ref, g_ref, wg_ref, wu_ref, wo_ref, g2_ref, *out_refs, mode):
    if mode == "emit_h":
        o_ref, hout_ref, h_ref = out_refs
    else:
        o_ref, h_ref = out_refs
    j = pl.program_id(1)

    @pl.when(j == 0)
    def _():
        x = x_ref[...]
        h_ref[...] = _rmsnorm(x, g_ref[...]).astype(BF16)
        o_ref[...] = x

    wg = wg_ref[...].astype(BF16)
    wu = wu_ref[...].astype(BF16)
    wo = wo_ref[...].astype(BF16)
    for r in range(TOKENS_PER_TILE // FFN_ROWS):
        rows = pl.ds(r * FFN_ROWS, FFN_ROWS)
        h = h_ref[rows, :]
        gate = _dot(h, wg)
        up = _dot(h, wu)
        act = (0.5 * (gate * _sigmoid(gate)) * up).astype(BF16)
        o_ref[rows, :] += _dot(act, wo)

    if mode != "plain":
        @pl.when(j == pl.num_programs(1) - 1)
        def _():
            y = _rmsnorm(o_ref[...], g2_ref[...])
            if mode == "emit_h":
                hout_ref[...] = y.astype(BF16)
            else:
                o_ref[...] = y


def _ffn_call(x, norm, w_in, w_out, g2, layer, g2_layer, mode):
    n_tok = x.shape[0]
    nt = n_tok // TOKENS_PER_TILE
    nf = D_FF // FFN_COLS
    tm = TOKENS_PER_TILE
    in_specs = [
        pl.BlockSpec((tm, D_MODEL), lambda i, j: (i, 0)),
        pl.BlockSpec((None, 1, D_MODEL), lambda i, j: (layer, 0, 0)),
        pl.BlockSpec((None, D_MODEL, FFN_COLS), lambda i, j: (layer, 0, j)),
        pl.BlockSpec((None, D_MODEL, FFN_COLS), lambda i, j: (layer, 0, j + nf)),
        pl.BlockSpec((None, FFN_COLS, D_MODEL), lambda i, j: (layer, j, 0)),
        pl.BlockSpec((None, 1, D_MODEL), lambda i, j: (g2_layer, 0, 0)),
    ]
    out_shape = [jax.ShapeDtypeStruct((n_tok, D_MODEL), F32)]
    out_specs = [pl.BlockSpec((tm, D_MODEL), lambda i, j: (i, 0))]
    if mode == "emit_h":
        out_shape.append(jax.ShapeDtypeStruct((n_tok, D_MODEL), BF16))
        out_specs.append(pl.BlockSpec((tm, D_MODEL), lambda i, j: (i, 0)))
    return pl.pallas_call(
        functools.partial(_ffn_kernel, mode=mode),
        grid=(nt, nf),
        in_specs=in_specs,
        out_specs=out_specs,
        out_shape=out_shape,
        scratch_shapes=[pltpu.VMEM((tm, D_MODEL), BF16)],
        compiler_params=pltpu.CompilerParams(
            dimension_semantics=("arbitrary", "arbitrary"),
            vmem_limit_bytes=VMEM_LIMIT_BYTES),
        name=f"ffn_{mode}",
    )(x, norm, w_in, w_in, w_out, g2)


N_PHASES = 6
_U0, _V0, _B10, _B20, _B30, _GA0, _GB0 = (k * N_MIX_BLOCKS for k in range(7))


def _phase(j):
    return j // N_MIX_BLOCKS, j % N_MIX_BLOCKS


def _w_in_a_block(j):
    p, n = _phase(j)
    first = jnp.where(p == 0, _V0, jnp.where(p == 1, _U0, jnp.where(
        p == 2, _GA0, jnp.where(p == 3, _B10, jnp.where(p == 4, _GB0, _V0)))))
    return first + jnp.where(p == 5, 0, n)


def _held_block(j, phase):
    return jnp.clip(j - phase * N_MIX_BLOCKS, 0, N_MIX_BLOCKS - 1)


def _mixer_kernel(h_ref, x_ref, wa_in_ref, wb_in_ref, wc_in_ref, sgn_ref, sgw_ref, sgb_ref,
                  cw_ref, wpa_ref, wpb_ref, wo_ref, o_ref,
                  vm_ref, vt_ref, s_ref, tb_ref):
    j = pl.program_id(1)
    phase = j // N_MIX_BLOCKS
    n = j % N_MIX_BLOCKS

    def proj_in(w_ref):
        return _dot(h_ref[...], w_ref[...].astype(BF16))

    def contract(lhs_ref, w_ref):
        acc = None
        for k in range(N_MIX_BLOCKS):
            w = w_ref[k * MIX_COLS:(k + 1) * MIX_COLS, :].astype(BF16)
            part = _dot(lhs_ref[k], w)
            acc = part if acc is None else acc + part
        return acc

    @pl.when(phase == 0)
    def _():
        vm_ref[n] = _gelu_tanh(proj_in(wa_in_ref))

    @pl.when(j == N_MIX_BLOCKS)
    def _():
        for c in range(N_CHUNKS):
            rows = pl.ds(c * CHUNK, CHUNK)
            ss = None
            for k in range(N_MIX_BLOCKS):
                v = vm_ref[k, rows, :]
                part = jnp.sum(v * v, axis=-1, keepdims=True)
                ss = part if ss is None else ss + part
            scale = lax.rsqrt(ss * (1.0 / D_MODEL) + EPS)
            for k in range(N_MIX_BLOCKS):
                g = sgn_ref[:, k * MIX_COLS:(k + 1) * MIX_COLS]
                vn = ((vm_ref[k, rows, :] * scale) * g).astype(BF16)
                for hh in range(HEADS_PER_BLOCK):
                    vt_ref[k * HEADS_PER_BLOCK + hh, :, c * CHUNK:(c + 1) * CHUNK] = (
                        vn[:, hh * HEAD_DIM:(hh + 1) * HEAD_DIM])

    @pl.when(phase == 1)
    def _():
        u = _gelu_tanh(proj_in(wa_in_ref))
        for hh in range(HEADS_PER_BLOCK):
            head = n * HEADS_PER_BLOCK + hh
            gated = _dot(sgw_ref[head].astype(BF16), vt_ref[head]) + sgb_ref[head]
            for c in range(N_CHUNKS):
                s_ref[n, c * CHUNK:(c + 1) * CHUNK, hh * HEAD_DIM:(hh + 1) * HEAD_DIM] = (
                    u[c * CHUNK:(c + 1) * CHUNK, hh * HEAD_DIM:(hh + 1) * HEAD_DIM]
                    * gated[:, c * CHUNK:(c + 1) * CHUNK]).astype(BF16)

    @pl.when(phase == 2)
    def _():
        vm_ref[n] = _sigmoid(proj_in(wa_in_ref)) * contract(s_ref, wpa_ref)

    @pl.when(phase == 3)
    def _():
        t = proj_in(wb_in_ref) * proj_in(wc_in_ref)
        row = lax.broadcasted_iota(jnp.int32, t.shape, 0)
        prev = jnp.where(row == 0, 0.0, pltpu.roll(t, 1, 0))
        nxt = jnp.where(row == SEQ - 1, 0.0, pltpu.roll(t, SEQ - 1, 0))
        conv = cw_ref[0, n] * prev + cw_ref[1, n] * t + cw_ref[2, n] * nxt
        tb_ref[n] = (proj_in(wa_in_ref) * conv).astype(BF16)

    @pl.when(phase == 4)
    def _():
        m = vm_ref[n] + _sigmoid(proj_in(wa_in_ref)) * contract(tb_ref, wpb_ref)
        s_ref[n] = m.astype(BF16)

    @pl.when(phase == 5)
    def _():
        o_ref[...] = x_ref[...] + contract(s_ref, wo_ref)


def _mixer_call(h, x, w_in, sgu_norm, sgu_w, sgu_b, conv_w, w_proj_a, w_proj_b, w_out, layer):
    n_tok = x.shape[0]
    nt = n_tok // SEQ
    nb = N_MIX_BLOCKS
    col_block = (None, D_MODEL, MIX_COLS)
    in_specs = [
        pl.BlockSpec((SEQ, D_MODEL), lambda i, j: (i, 0)),
        pl.BlockSpec((SEQ, MIX_COLS), lambda i, j: (i, _held_block(j, 5))),
        pl.BlockSpec(col_block, lambda i, j: (layer, 0, _w_in_a_block(j))),
        pl.BlockSpec(col_block, lambda i, j: (layer, 0, _B20 + _held_block(j, 3))),
        pl.BlockSpec(col_block, lambda i, j: (layer, 0, _B30 + _held_block(j, 3))),
        pl.BlockSpec((None, 1, D_MODEL), lambda i, j: (layer, 0, 0)),
        pl.BlockSpec((None, HEADS, CHUNK, CHUNK), lambda i, j: (layer, 0, 0, 0)),
        pl.BlockSpec((None, HEADS, CHUNK, 1), lambda i, j: (layer, 0, 0, 0)),
        pl.BlockSpec((None, 3, nb, 1, MIX_COLS), lambda i, j: (layer, 0, 0, 0, 0)),
        pl.BlockSpec(col_block, lambda i, j: (layer, 0, _held_block(j, 2))),
        pl.BlockSpec(col_block, lambda i, j: (layer, 0, _held_block(j, 4))),
        pl.BlockSpec(col_block, lambda i, j: (layer, 0, _held_block(j, 5))),
    ]
    return pl.pallas_call(
        _mixer_kernel,
        grid=(nt, N_PHASES * nb),
        in_specs=in_specs,
        out_specs=pl.BlockSpec((SEQ, MIX_COLS), lambda i, j: (i, _held_block(j, 5))),
        out_shape=jax.ShapeDtypeStruct((n_tok, D_MODEL), F32),
        scratch_shapes=[
            pltpu.VMEM((nb, SEQ, MIX_COLS), F32),
            pltpu.VMEM((HEADS, CHUNK, SEQ), BF16),
            pltpu.VMEM((nb, SEQ, MIX_COLS), BF16),
            pltpu.VMEM((nb, SEQ, MIX_COLS), BF16),
        ],
        compiler_params=pltpu.CompilerParams(
            dimension_semantics=("arbitrary", "arbitrary"),
            vmem_limit_bytes=VMEM_LIMIT_BYTES),
        name="mixer",
    )(h, x, w_in, w_in, w_in, sgu_norm, sgu_w, sgu_b, conv_w, w_proj_a, w_proj_b, w_out)


def kernel(x, ffn1_norm, ffn1_w_in, ffn1_w_out, mix_norm, w_in, sgu_norm, sgu_w, sgu_b, conv_w,
           w_proj_a, w_proj_b, w_out, ffn2_norm, ffn2_w_in, ffn2_w_out, final_norm):
    batch, seq, d = x.shape
    depth = ffn1_norm.shape[0]
    assert (seq, d) == (SEQ, D_MODEL) and w_in.shape[-1] == N_IN and ffn1_w_out.shape[1] == D_FF
    xt = x.reshape(batch * seq, d)
    gains = lambda g: g.reshape(depth, 1, d)
    ffn1_g, mix_g, ffn2_g, sgu_g = gains(ffn1_norm), gains(mix_norm), gains(ffn2_norm), gains(sgu_norm)
    final_g = final_norm.reshape(1, 1, d)
    sgu_b4 = sgu_b.reshape(depth, HEADS, CHUNK, 1)
    conv_w5 = conv_w.reshape(depth, 3, N_MIX_BLOCKS, 1, MIX_COLS)
    for l in range(depth):
        xt, h = _ffn_call(xt, ffn1_g, ffn1_w_in, ffn1_w_out, mix_g, l, l, "emit_h")
        xt = _mixer_call(h, xt, w_in, sgu_g, sgu_w, sgu_b4, conv_w5, w_proj_a, w_proj_b, w_out, l)
        if l == depth - 1:
            (xt,) = _ffn_call(xt, ffn2_g, ffn2_w_in, ffn2_w_out, final_g, l, 0, "final")
        else:
            (xt,) = _ffn_call(xt, ffn2_g, ffn2_w_in, ffn2_w_out, ffn2_g, l, l, "plain")
    return xt.reshape(batch, seq, d)
```

```python
import functools

import numpy as np
import jax
import jax.numpy as jnp
from jax import lax
from jax.experimental import pallas as pl
from jax.experimental.pallas import tpu as pltpu

F32 = jnp.float32
BF16 = jnp.bfloat16
EPS = 1e-6

D_MODEL = 1024
SEQ = 2048
CHUNK = 128
HEADS = 8
HEAD_DIM = 128
D_FF = 2816
N_IN = 7 * D_MODEL

TOKENS_PER_TILE = SEQ
FFN_COLS = 256
FFN_ROWS = 512
N_FFN_CHUNKS = D_FF // FFN_COLS
FFN_SLOTS = 3
MIX_COLS = 256
N_MIX_BLOCKS = D_MODEL // MIX_COLS
N_CHUNKS = SEQ // CHUNK
HEADS_PER_BLOCK = MIX_COLS // HEAD_DIM
VMEM_LIMIT_BYTES = 60 * 1024 * 1024


def _rmsnorm(x, g):
    ms = jnp.mean(x * x, axis=-1, keepdims=True)
    return (x * lax.rsqrt(ms + EPS)) * g


def _gelu_tanh(x):
    c = np.float32(np.sqrt(2.0 / np.pi))
    return x * (0.5 * (1.0 + jnp.tanh(c * (x + 0.044715 * (x * x * x)))))


def _sigmoid(x):
    return 1.0 / (1.0 + jnp.exp(-x))


def _dot(a, b):
    return jnp.dot(a, b, preferred_element_type=F32)


def _ffn_weight_copies(w_in_hbm, w_out_hbm, wg_buf, wu_buf, wo_buf, sem, layer, chunk):
    slot = chunk % FFN_SLOTS
    cols = pl.ds(chunk * FFN_COLS, FFN_COLS)
    up_cols = pl.ds(D_FF + chunk * FFN_COLS, FFN_COLS)
    return (
        pltpu.make_async_copy(w_in_hbm.at[layer, :, cols], wg_buf.at[slot], sem.at[0, slot]),
        pltpu.make_async_copy(w_in_hbm.at[layer, :, up_cols], wu_buf.at[slot], sem.at[1, slot]),
        pltpu.make_async_copy(w_out_hbm.at[layer, cols, :], wo_buf.at[slot], sem.at[2, slot]),
    )


def _ffn_kernel(x_ref, g_ref, g2_ref, w_in_hbm, w_out_hbm, *refs, mode, layer):
    if mode == "emit_h":
        o_ref, hout_ref, h_ref, wg_buf, wu_buf, wo_buf, sem = refs
    else:
        o_ref, h_ref, wg_buf, wu_buf, wo_buf, sem = refs
    i = pl.program_id(0)
    copies = functools.partial(_ffn_weight_copies, w_in_hbm, w_out_hbm, wg_buf, wu_buf, wo_buf,
                               sem, layer)
    n_row_chunks = TOKENS_PER_TILE // FFN_ROWS

    @pl.when(i == 0)
    def _():
        for cp in copies(0):
            cp.start()

    for r in range(n_row_chunks):
        rows = pl.ds(r * FFN_ROWS, FFN_ROWS)
        h_ref[rows, :] = _rmsnorm(x_ref[rows, :], g_ref[...]).astype(BF16)

    for c in range(N_FFN_CHUNKS):
        if c + 1 < N_FFN_CHUNKS:
            for cp in copies(c + 1):
                cp.start()
        for cp in copies(c):
            cp.wait()
        slot = c % FFN_SLOTS
        wg = wg_buf[slot].astype(BF16)
        wu = wu_buf[slot].astype(BF16)
        wo = wo_buf[slot].astype(BF16)
        for r in range(n_row_chunks):
            rows = pl.ds(r * FFN_ROWS, FFN_ROWS)
            h = h_ref[rows, :]
            gate = _dot(h, wg)
            up = _dot(h, wu)
            act = (0.5 * (gate * _sigmoid(gate)) * up).astype(BF16)
            base = x_ref[rows, :] if c == 0 else o_ref[rows, :]
            o_ref[rows, :] = base + _dot(act, wo)

    @pl.when(i + 1 < pl.num_programs(0))
    def _():
        for cp in copies(0):
            cp.start()

    if mode != "plain":
        for r in range(n_row_chunks):
            rows = pl.ds(r * FFN_ROWS, FFN_ROWS)
            y = _rmsnorm(o_ref[rows, :], g2_ref[...])
            if mode == "emit_h":
                hout_ref[rows, :] = y.astype(BF16)
            else:
                o_ref[rows, :] = y


def _ffn_call(x, norm, w_in, w_out, g2, layer, g2_layer, mode):
    n_tok = x.shape[0]
    nt = n_tok // TOKENS_PER_TILE
    tm = TOKENS_PER_TILE
    in_specs = [
        pl.BlockSpec((tm, D_MODEL), lambda i: (i, 0)),
        pl.BlockSpec((None, 1, D_MODEL), lambda i: (layer, 0, 0)),
        pl.BlockSpec((None, 1, D_MODEL), lambda i: (g2_layer, 0, 0)),
        pl.BlockSpec(memory_space=pl.ANY),
        pl.BlockSpec(memory_space=pl.ANY),
    ]
    out_shape = [jax.ShapeDtypeStruct((n_tok, D_MODEL), F32)]
    out_specs = [pl.BlockSpec((tm, D_MODEL), lambda i: (i, 0))]
    if mode == "emit_h":
        out_shape.append(jax.ShapeDtypeStruct((n_tok, D_MODEL), BF16))
        out_specs.append(pl.BlockSpec((tm, D_MODEL), lambda i: (i, 0)))
    return pl.pallas_call(
        functools.partial(_ffn_kernel, mode=mode, layer=layer),
        grid=(nt,),
        in_specs=in_specs,
        out_specs=out_specs,
        out_shape=out_shape,
        scratch_shapes=[
            pltpu.VMEM((tm, D_MODEL), BF16),
            pltpu.VMEM((FFN_SLOTS, D_MODEL, FFN_COLS), F32),
            pltpu.VMEM((FFN_SLOTS, D_MODEL, FFN_COLS), F32),
            pltpu.VMEM((FFN_SLOTS, FFN_COLS, D_MODEL), F32),
            pltpu.SemaphoreType.DMA((3, FFN_SLOTS)),
        ],
        compiler_params=pltpu.CompilerParams(
            dimension_semantics=("arbitrary",),
            vmem_limit_bytes=VMEM_LIMIT_BYTES),
        name=f"ffn_{mode}",
    )(x, norm, g2, w_in, w_out)


N_PHASES = 6
_U0, _V0, _B10, _B20, _B30, _GA0, _GB0 = (k * N_MIX_BLOCKS for k in range(7))


def _phase(j):
    return j // N_MIX_BLOCKS, j % N_MIX_BLOCKS


def _w_in_a_block(j):
    p, n = _phase(j)
    first = jnp.where(p == 0, _V0, jnp.where(p == 1, _U0, jnp.where(
        p == 2, _GA0, jnp.where(p == 3, _B10, jnp.where(p == 4, _GB0, _V0)))))
    return first + jnp.where(p == 5, 0, n)


def _held_block(j, phase):
    return jnp.clip(j - phase * N_MIX_BLOCKS, 0, N_MIX_BLOCKS - 1)


def _mixer_kernel(h_ref, x_ref, wa_in_ref, wb_in_ref, wc_in_ref, sgn_ref, sgw_ref, sgb_ref,
                  cw_ref, wpa_ref, wpb_ref, wo_ref, o_ref,
                  vm_ref, vt_ref, s_ref, tb_ref):
    j = pl.program_id(1)
    phase = j // N_MIX_BLOCKS
    n = j % N_MIX_BLOCKS

    def proj_in(w_ref):
        return _dot(h_ref[...], w_ref[...].astype(BF16))

    def contract(lhs_ref, w_ref):
        acc = None
        for k in range(N_MIX_BLOCKS):
            w = w_ref[k * MIX_COLS:(k + 1) * MIX_COLS, :].astype(BF16)
            part = _dot(lhs_ref[k], w)
            acc = part if acc is None else acc + part
        return acc

    @pl.when(phase == 0)
    def _():
        vm_ref[n] = _gelu_tanh(proj_in(wa_in_ref))

    @pl.when(j == N_MIX_BLOCKS)
    def _():
        for c in range(N_CHUNKS):
            rows = pl.ds(c * CHUNK, CHUNK)
            ss = None
            for k in range(N_MIX_BLOCKS):
                v = vm_ref[k, rows, :]
                part = jnp.sum(v * v, axis=-1, keepdims=True)
                ss = part if ss is None else ss + part
            scale = lax.rsqrt(ss * (1.0 / D_MODEL) + EPS)
            for k in range(N_MIX_BLOCKS):
                g = sgn_ref[:, k * MIX_COLS:(k + 1) * MIX_COLS]
                vn = ((vm_ref[k, rows, :] * scale) * g).astype(BF16)
                for hh in range(HEADS_PER_BLOCK):
                    vt_ref[k * HEADS_PER_BLOCK + hh, :, c * CHUNK:(c + 1) * CHUNK] = (
                        vn[:, hh * HEAD_DIM:(hh + 1) * HEAD_DIM])

    @pl.when(phase == 1)
    def _():
        u = _gelu_tanh(proj_in(wa_in_ref))
        for hh in range(HEADS_PER_BLOCK):
            head = n * HEADS_PER_BLOCK + hh
            gated = _dot(sgw_ref[head].astype(BF16), vt_ref[head]) + sgb_ref[head]
            for c in range(N_CHUNKS):
                s_ref[n, c * CHUNK:(c + 1) * CHUNK, hh * HEAD_DIM:(hh + 1) * HEAD_DIM] = (
                    u[c * CHUNK:(c + 1) * CHUNK, hh * HEAD_DIM:(hh + 1) * HEAD_DIM]
                    * gated[:, c * CHUNK:(c + 1) * CHUNK]).astype(BF16)

    @pl.when(phase == 2)
    def _():
        vm_ref[n] = _sigmoid(proj_in(wa_in_ref)) * contract(s_ref, wpa_ref)

    @pl.when(phase == 3)
    def _():
        t = proj_in(wb_in_ref) * proj_in(wc_in_ref)
        row = lax.broadcasted_iota(jnp.int32, t.shape, 0)
        prev = jnp.where(row == 0, 0.0, pltpu.roll(t, 1, 0))
        nxt = jnp.where(row == SEQ - 1, 0.0, pltpu.roll(t, SEQ - 1, 0))
        conv = cw_ref[0, n] * prev + cw_ref[1, n] * t + cw_ref[2, n] * nxt
        tb_ref[n] = (proj_in(wa_in_ref) * conv).astype(BF16)

    @pl.when(phase == 4)
    def _():
        m = vm_ref[n] + _sigmoid(proj_in(wa_in_ref)) * contract(tb_ref, wpb_ref)
        s_ref[n] = m.astype(BF16)

    @pl.when(phase == 5)
    def _():
        o_ref[...] = x_ref[...] + contract(s_ref, wo_ref)


def _mixer_call(h, x, w_in, sgu_norm, sgu_w, sgu_b, conv_w, w_proj_a, w_proj_b, w_out, layer):
    n_tok = x.shape[0]
    nt = n_tok // SEQ
    nb = N_MIX_BLOCKS
    col_block = (None, D_MODEL, MIX_COLS)
    in_specs = [
        pl.BlockSpec((SEQ, D_MODEL), lambda i, j: (i, 0)),
        pl.BlockSpec((SEQ, MIX_COLS), lambda i, j: (i, _held_block(j, 5))),
        pl.BlockSpec(col_block, lambda i, j: (layer, 0, _w_in_a_block(j))),
        pl.BlockSpec(col_block, lambda i, j: (layer, 0, _B20 + _held_block(j, 3))),
        pl.BlockSpec(col_block, lambda i, j: (layer, 0, _B30 + _held_block(j, 3))),
        pl.BlockSpec((None, 1, D_MODEL), lambda i, j: (layer, 0, 0)),
        pl.BlockSpec((None, HEADS, CHUNK, CHUNK), lambda i, j: (layer, 0, 0, 0)),
        pl.BlockSpec((None, HEADS, CHUNK, 1), lambda i, j: (layer, 0, 0, 0)),
        pl.BlockSpec((None, 3, nb, 1, MIX_COLS), lambda i, j: (layer, 0, 0, 0, 0)),
        pl.BlockSpec(col_block, lambda i, j: (layer, 0, _held_block(j, 2))),
        pl.BlockSpec(col_block, lambda i, j: (layer, 0, _held_block(j, 4))),
        pl.BlockSpec(col_block, lambda i, j: (layer, 0, _held_block(j, 5))),
    ]
    return pl.pallas_call(
        _mixer_kernel,
        grid=(nt, N_PHASES * nb),
        in_specs=in_specs,
        out_specs=pl.BlockSpec((SEQ, MIX_COLS), lambda i, j: (i, _held_block(j, 5))),
        out_shape=jax.ShapeDtypeStruct((n_tok, D_MODEL), F32),
        scratch_shapes=[
            pltpu.VMEM((nb, SEQ, MIX_COLS), F32),
            pltpu.VMEM((HEADS, CHUNK, SEQ), BF16),
            pltpu.VMEM((nb, SEQ, MIX_COLS), BF16),
            pltpu.VMEM((nb, SEQ, MIX_COLS), BF16),
        ],
        compiler_params=pltpu.CompilerParams(
            dimension_semantics=("arbitrary", "arbitrary"),
            vmem_limit_bytes=VMEM_LIMIT_BYTES),
        name="mixer",
    )(h, x, w_in, w_in, w_in, sgu_norm, sgu_w, sgu_b, conv_w, w_proj_a, w_proj_b, w_out)


def kernel(x, ffn1_norm, ffn1_w_in, ffn1_w_out, mix_norm, w_in, sgu_norm, sgu_w, sgu_b, conv_w,
           w_proj_a, w_proj_b, w_out, ffn2_norm, ffn2_w_in, ffn2_w_out, final_norm):
    batch, seq, d = x.shape
    depth = ffn1_norm.shape[0]
    assert (seq, d) == (SEQ, D_MODEL) and w_in.shape[-1] == N_IN and ffn1_w_out.shape[1] == D_FF
    xt = x.reshape(batch * seq, d)
    gains = lambda g: g.reshape(depth, 1, d)
    ffn1_g, mix_g, ffn2_g, sgu_g = gains(ffn1_norm), gains(mix_norm), gains(ffn2_norm), gains(sgu_norm)
    final_g = final_norm.reshape(1, 1, d)
    sgu_b4 = sgu_b.reshape(depth, HEADS, CHUNK, 1)
    conv_w5 = conv_w.reshape(depth, 3, N_MIX_BLOCKS, 1, MIX_COLS)
    for l in range(depth):
        xt, h = _ffn_call(xt, ffn1_g, ffn1_w_in, ffn1_w_out, mix_g, l, l, "emit_h")
        xt = _mixer_call(h, xt, w_in, sgu_g, sgu_w, sgu_b4, conv_w5, w_proj_a, w_proj_b, w_out, l)
        if l == depth - 1:
            (xt,) = _ffn_call(xt, ffn2_g, ffn2_w_in, ffn2_w_out, final_g, l, 0, "final")
        else:
            (xt,) = _ffn_call(xt, ffn2_g, ffn2_w_in, ffn2_w_out, ffn2_g, l, l, "plain")
    return xt.reshape(batch, seq, d)
```

```python
import functools

import numpy as np
import jax
import jax.numpy as jnp
from jax import lax
from jax.experimental import pallas as pl
from jax.experimental.pallas import tpu as pltpu

F32 = jnp.float32
BF16 = jnp.bfloat16
EPS = 1e-6

D_MODEL = 1024
SEQ = 2048
CHUNK = 128
HEADS = 8
HEAD_DIM = 128
D_FF = 2816
N_IN = 7 * D_MODEL

TOKENS_PER_TILE = SEQ
FFN_COLS = 256
FFN_ROWS = 512
N_FFN_CHUNKS = D_FF // FFN_COLS
FFN_GROUP = 2
N_FFN_GROUPS = -(-N_FFN_CHUNKS // FFN_GROUP)
FFN_SLOTS = 2
assert N_FFN_GROUPS % FFN_SLOTS == 0
MIX_COLS = 256
N_MIX_BLOCKS = D_MODEL // MIX_COLS
N_CHUNKS = SEQ // CHUNK
HEADS_PER_BLOCK = MIX_COLS // HEAD_DIM
VMEM_LIMIT_BYTES = 60 * 1024 * 1024


def _rmsnorm(x, g):
    ms = jnp.mean(x * x, axis=-1, keepdims=True)
    return (x * lax.rsqrt(ms + EPS)) * g


def _gelu_tanh(x):
    c = np.float32(np.sqrt(2.0 / np.pi))
    return x * (0.5 * (1.0 + jnp.tanh(c * (x + 0.044715 * (x * x * x)))))


def _sigmoid(x):
    return 1.0 / (1.0 + jnp.exp(-x))


def _dot(a, b):
    return jnp.dot(a, b, preferred_element_type=F32)


def _ffn_group_chunks(group):
    return range(group * FFN_GROUP, min((group + 1) * FFN_GROUP, N_FFN_CHUNKS))


def _ffn_group_copies(w_in_hbm, w_out_hbm, wg_buf, wu_buf, wo_buf, sem, layer, group):
    slot = group % FFN_SLOTS
    copies = []
    for k, chunk in enumerate(_ffn_group_chunks(group)):
        cols = pl.ds(chunk * FFN_COLS, FFN_COLS)
        up_cols = pl.ds(D_FF + chunk * FFN_COLS, FFN_COLS)
        copies += [
            pltpu.make_async_copy(w_in_hbm.at[layer, :, cols], wg_buf.at[slot, k], sem.at[slot]),
            pltpu.make_async_copy(w_in_hbm.at[layer, :, up_cols], wu_buf.at[slot, k], sem.at[slot]),
            pltpu.make_async_copy(w_out_hbm.at[layer, cols, :], wo_buf.at[slot, k], sem.at[slot]),
        ]
    return copies


def _ffn_kernel(x_ref, g_ref, g2_ref, w_in_hbm, w_out_hbm, *refs, mode, layer):
    if mode == "emit_h":
        o_ref, hout_ref, h_ref, wg_buf, wu_buf, wo_buf, sem = refs
    else:
        o_ref, h_ref, wg_buf, wu_buf, wo_buf, sem = refs
    i = pl.program_id(0)
    n_tiles = pl.num_programs(0)
    copies = functools.partial(_ffn_group_copies, w_in_hbm, w_out_hbm, wg_buf, wu_buf, wo_buf,
                               sem, layer)
    n_row_chunks = TOKENS_PER_TILE // FFN_ROWS

    def start(group):
        for cp in copies(group):
            cp.start()

    def wait(group):
        for cp in copies(group):
            cp.wait()

    pl.when(i == 0)(lambda: start(0))

    for g in range(N_FFN_GROUPS):
        wait(g)
        if g + 1 < N_FFN_GROUPS:
            start(g + 1)
        else:
            pl.when(i + 1 < n_tiles)(lambda: start(0))
        if g == 0:
            for r in range(n_row_chunks):
                rows = pl.ds(r * FFN_ROWS, FFN_ROWS)
                h_ref[rows, :] = _rmsnorm(x_ref[rows, :], g_ref[...]).astype(BF16)
        slot = g % FFN_SLOTS
        for k, c in enumerate(_ffn_group_chunks(g)):
            wg = wg_buf[slot, k].astype(BF16)
            wu = wu_buf[slot, k].astype(BF16)
            wo = wo_buf[slot, k].astype(BF16)
            for r in range(n_row_chunks):
                rows = pl.ds(r * FFN_ROWS, FFN_ROWS)
                h = h_ref[rows, :]
                gate = _dot(h, wg)
                up = _dot(h, wu)
                act = (0.5 * (gate * _sigmoid(gate)) * up).astype(BF16)
                base = x_ref[rows, :] if c == 0 else o_ref[rows, :]
                o_ref[rows, :] = base + _dot(act, wo)

    if mode != "plain":
        for r in range(n_row_chunks):
            rows = pl.ds(r * FFN_ROWS, FFN_ROWS)
            y = _rmsnorm(o_ref[rows, :], g2_ref[...])
            if mode == "emit_h":
                hout_ref[rows, :] = y.astype(BF16)
            else:
                o_ref[rows, :] = y


def _ffn_call(x, norm, w_in, w_out, g2, layer, g2_layer, mode):
    n_tok = x.shape[0]
    nt = n_tok // TOKENS_PER_TILE
    tm = TOKENS_PER_TILE
    in_specs = [
        pl.BlockSpec((tm, D_MODEL), lambda i: (i, 0)),
        pl.BlockSpec((None, 1, D_MODEL), lambda i: (layer, 0, 0)),
        pl.BlockSpec((None, 1, D_MODEL), lambda i: (g2_layer, 0, 0)),
        pl.BlockSpec(memory_space=pl.ANY),
        pl.BlockSpec(memory_space=pl.ANY),
    ]
    out_shape = [jax.ShapeDtypeStruct((n_tok, D_MODEL), F32)]
    out_specs = [pl.BlockSpec((tm, D_MODEL), lambda i: (i, 0))]
    if mode == "emit_h":
        out_shape.append(jax.ShapeDtypeStruct((n_tok, D_MODEL), BF16))
        out_specs.append(pl.BlockSpec((tm, D_MODEL), lambda i: (i, 0)))
    return pl.pallas_call(
        functools.partial(_ffn_kernel, mode=mode, layer=layer),
        grid=(nt,),
        in_specs=in_specs,
        out_specs=out_specs,
        out_shape=out_shape,
        scratch_shapes=[
            pltpu.VMEM((tm, D_MODEL), BF16),
            pltpu.VMEM((FFN_SLOTS, FFN_GROUP, D_MODEL, FFN_COLS), F32),
            pltpu.VMEM((FFN_SLOTS, FFN_GROUP, D_MODEL, FFN_COLS), F32),
            pltpu.VMEM((FFN_SLOTS, FFN_GROUP, FFN_COLS, D_MODEL), F32),
            pltpu.SemaphoreType.DMA((FFN_SLOTS,)),
        ],
        compiler_params=pltpu.CompilerParams(
            dimension_semantics=("arbitrary",),
            vmem_limit_bytes=VMEM_LIMIT_BYTES),
        name=f"ffn_{mode}",
    )(x, norm, g2, w_in, w_out)


N_PHASES = 6
_U0, _V0, _B10, _B20, _B30, _GA0, _GB0 = (k * N_MIX_BLOCKS for k in range(7))


def _phase(j):
    return j // N_MIX_BLOCKS, j % N_MIX_BLOCKS


def _w_in_a_block(j):
    p, n = _phase(j)
    first = jnp.where(p == 0, _V0, jnp.where(p == 1, _U0, jnp.where(
        p == 2, _GA0, jnp.where(p == 3, _B10, jnp.where(p == 4, _GB0, _V0)))))
    return first + jnp.where(p == 5, 0, n)


def _held_block(j, phase):
    return jnp.clip(j - phase * N_MIX_BLOCKS, 0, N_MIX_BLOCKS - 1)


def _mixer_kernel(h_ref, x_ref, wa_in_ref, wb_in_ref, wc_in_ref, sgn_ref, sgw_ref, sgb_ref,
                  cw_ref, wpa_ref, wpb_ref, wo_ref, o_ref,
                  vm_ref, vt_ref, s_ref, tb_ref):
    j = pl.program_id(1)
    phase = j // N_MIX_BLOCKS
    n = j % N_MIX_BLOCKS

    def proj_in(w_ref):
        return _dot(h_ref[...], w_ref[...].astype(BF16))

    def contract(lhs_ref, w_ref):
        acc = None
        for k in range(N_MIX_BLOCKS):
            w = w_ref[k * MIX_COLS:(k + 1) * MIX_COLS, :].astype(BF16)
            part = _dot(lhs_ref[k], w)
            acc = part if acc is None else acc + part
        return acc

    @pl.when(phase == 0)
    def _():
        vm_ref[n] = _gelu_tanh(proj_in(wa_in_ref))

    @pl.when(j == N_MIX_BLOCKS)
    def _():
        for c in range(N_CHUNKS):
            rows = pl.ds(c * CHUNK, CHUNK)
            ss = None
            for k in range(N_MIX_BLOCKS):
                v = vm_ref[k, rows, :]
                part = jnp.sum(v * v, axis=-1, keepdims=True)
                ss = part if ss is None else ss + part
            scale = lax.rsqrt(ss * (1.0 / D_MODEL) + EPS)
            for k in range(N_MIX_BLOCKS):
                g = sgn_ref[:, k * MIX_COLS:(k + 1) * MIX_COLS]
                vn = ((vm_ref[k, rows, :] * scale) * g).astype(BF16)
                for hh in range(HEADS_PER_BLOCK):
                    vt_ref[k * HEADS_PER_BLOCK + hh, :, c * CHUNK:(c + 1) * CHUNK] = (
                        vn[:, hh * HEAD_DIM:(hh + 1) * HEAD_DIM])

    @pl.when(phase == 1)
    def _():
        u = _gelu_tanh(proj_in(wa_in_ref))
        for hh in range(HEADS_PER_BLOCK):
            head = n * HEADS_PER_BLOCK + hh
            gated = _dot(sgw_ref[head].astype(BF16), vt_ref[head]) + sgb_ref[head]
            for c in range(N_CHUNKS):
                s_ref[n, c * CHUNK:(c + 1) * CHUNK, hh * HEAD_DIM:(hh + 1) * HEAD_DIM] = (
                    u[c * CHUNK:(c + 1) * CHUNK, hh * HEAD_DIM:(hh + 1) * HEAD_DIM]
                    * gated[:, c * CHUNK:(c + 1) * CHUNK]).astype(BF16)

    @pl.when(phase == 2)
    def _():
        vm_ref[n] = _sigmoid(proj_in(wa_in_ref)) * contract(s_ref, wpa_ref)

    @pl.when(phase == 3)
    def _():
        t = proj_in(wb_in_ref) * proj_in(wc_in_ref)
        row = lax.broadcasted_iota(jnp.int32, t.shape, 0)
        prev = jnp.where(row == 0, 0.0, pltpu.roll(t, 1, 0))
        nxt = jnp.where(row == SEQ - 1, 0.0, pltpu.roll(t, SEQ - 1, 0))
        conv = cw_ref[0, n] * prev + cw_ref[1, n] * t + cw_ref[2, n] * nxt
        tb_ref[n] = (proj_in(wa_in_ref) * conv).astype(BF16)

    @pl.when(phase == 4)
    def _():
        m = vm_ref[n] + _sigmoid(proj_in(wa_in_ref)) * contract(tb_ref, wpb_ref)
        s_ref[n] = m.astype(BF16)

    @pl.when(phase == 5)
    def _():
        o_ref[...] = x_ref[...] + contract(s_ref, wo_ref)


def _mixer_call(h, x, w_in, sgu_norm, sgu_w, sgu_b, conv_w, w_proj_a, w_proj_b, w_out, layer):
    n_tok = x.shape[0]
    nt = n_tok // SEQ
    nb = N_MIX_BLOCKS
    col_block = (None, D_MODEL, MIX_COLS)
    in_specs = [
        pl.BlockSpec((SEQ, D_MODEL), lambda i, j: (i, 0)),
        pl.BlockSpec((SEQ, MIX_COLS), lambda i, j: (i, _held_block(j, 5))),
        pl.BlockSpec(col_block, lambda i, j: (layer, 0, _w_in_a_block(j))),
        pl.BlockSpec(col_block, lambda i, j: (layer, 0, _B20 + _held_block(j, 3))),
        pl.BlockSpec(col_block, lambda i, j: (layer, 0, _B30 + _held_block(j, 3))),
        pl.BlockSpec((None, 1, D_MODEL), lambda i, j: (layer, 0, 0)),
        pl.BlockSpec((None, HEADS, CHUNK, CHUNK), lambda i, j: (layer, 0, 0, 0)),
        pl.BlockSpec((None, HEADS, CHUNK, 1), lambda i, j: (layer, 0, 0, 0)),
        pl.BlockSpec((None, 3, nb, 1, MIX_COLS), lambda i, j: (layer, 0, 0, 0, 0)),
        pl.BlockSpec(col_block, lambda i, j: (layer, 0, _held_block(j, 2))),
        pl.BlockSpec(col_block, lambda i, j: (layer, 0, _held_block(j, 4))),
        pl.BlockSpec(col_block, lambda i, j: (layer, 0, _held_block(j, 5))),
    ]
    return pl.pallas_call(
        _mixer_kernel,
        grid=(nt, N_PHASES * nb),
        in_specs=in_specs,
        out_specs=pl.BlockSpec((SEQ, MIX_COLS), lambda i, j: (i, _held_block(j, 5))),
        out_shape=jax.ShapeDtypeStruct((n_tok, D_MODEL), F32),
        scratch_shapes=[
            pltpu.VMEM((nb, SEQ, MIX_COLS), F32),
            pltpu.VMEM((HEADS, CHUNK, SEQ), BF16),
            pltpu.VMEM((nb, SEQ, MIX_COLS), BF16),
            pltpu.VMEM((nb, SEQ, MIX_COLS), BF16),
        ],
        compiler_params=pltpu.CompilerParams(
            dimension_semantics=("arbitrary", "arbitrary"),
            vmem_limit_bytes=VMEM_LIMIT_BYTES),
        name="mixer",
    )(h, x, w_in, w_in, w_in, sgu_norm, sgu_w, sgu_b, conv_w, w_proj_a, w_proj_b, w_out)


def kernel(x, ffn1_norm, ffn1_w_in, ffn1_w_out, mix_norm, w_in, sgu_norm, sgu_w, sgu_b, conv_w,
           w_proj_a, w_proj_b, w_out, ffn2_norm, ffn2_w_in, ffn2_w_out, final_norm):
    batch, seq, d = x.shape
    depth = ffn1_norm.shape[0]
    assert (seq, d) == (SEQ, D_MODEL) and w_in.shape[-1] == N_IN and ffn1_w_out.shape[1] == D_FF
    xt = x.reshape(batch * seq, d)
    gains = lambda g: g.reshape(depth, 1, d)
    ffn1_g, mix_g, ffn2_g, sgu_g = gains(ffn1_norm), gains(mix_norm), gains(ffn2_norm), gains(sgu_norm)
    final_g = final_norm.reshape(1, 1, d)
    sgu_b4 = sgu_b.reshape(depth, HEADS, CHUNK, 1)
    conv_w5 = conv_w.reshape(depth, 3, N_MIX_BLOCKS, 1, MIX_COLS)
    for l in range(depth):
        xt, h = _ffn_call(xt, ffn1_g, ffn1_w_in, ffn1_w_out, mix_g, l, l, "emit_h")
        xt = _mixer_call(h, xt, w_in, sgu_g, sgu_w, sgu_b4, conv_w5, w_proj_a, w_proj_b, w_out, l)
        if l == depth - 1:
            (xt,) = _ffn_call(xt, ffn2_g, ffn2_w_in, ffn2_w_out, final_g, l, 0, "final")
        else:
            (xt,) = _ffn_call(xt, ffn2_g, ffn2_w_in, ffn2_w_out, ffn2_g, l, l, "plain")
    return xt.reshape(batch, seq, d)
```

```python
import functools

import numpy as np
import jax
import jax.numpy as jnp
from jax import lax
from jax.experimental import pallas as pl
from jax.experimental.pallas import tpu as pltpu

F32 = jnp.float32
BF16 = jnp.bfloat16
EPS = 1e-6

D_MODEL = 1024
SEQ = 2048
CHUNK = 128
HEADS = 8
HEAD_DIM = 128
D_FF = 2816
N_IN = 7 * D_MODEL

TOKENS_PER_TILE = SEQ
FFN_COLS = 256
FFN_ROWS = 512
N_FFN_CHUNKS = D_FF // FFN_COLS
FFN_GROUP = 2
N_FFN_GROUPS = -(-N_FFN_CHUNKS // FFN_GROUP)
FFN_SLOTS = 2
assert N_FFN_GROUPS % FFN_SLOTS == 0
MIX_COLS = 256
N_MIX_BLOCKS = D_MODEL // MIX_COLS
N_CHUNKS = SEQ // CHUNK
HEADS_PER_BLOCK = MIX_COLS // HEAD_DIM
VMEM_LIMIT_BYTES = 60 * 1024 * 1024


def _rmsnorm(x, g):
    ms = jnp.mean(x * x, axis=-1, keepdims=True)
    return (x * lax.rsqrt(ms + EPS)) * g


def _gelu_tanh(x):
    c = np.float32(np.sqrt(2.0 / np.pi))
    return x * (0.5 * (1.0 + jnp.tanh(c * (x + 0.044715 * (x * x * x)))))


def _sigmoid(x):
    return 1.0 / (1.0 + jnp.exp(-x))


def _dot(a, b):
    return jnp.dot(a, b, preferred_element_type=F32)


def _ffn_group_chunks(group):
    return range(group * FFN_GROUP, min((group + 1) * FFN_GROUP, N_FFN_CHUNKS))


def _ffn_group_copies(w_in_hbm, w_out_hbm, wg_buf, wu_buf, wo_buf, sem, layer, group):
    slot = group % FFN_SLOTS
    copies = []
    for k, chunk in enumerate(_ffn_group_chunks(group)):
        cols = pl.ds(chunk * FFN_COLS, FFN_COLS)
        up_cols = pl.ds(D_FF + chunk * FFN_COLS, FFN_COLS)
        copies += [
            pltpu.make_async_copy(w_in_hbm.at[layer, :, cols], wg_buf.at[slot, k], sem.at[slot]),
            pltpu.make_async_copy(w_in_hbm.at[layer, :, up_cols], wu_buf.at[slot, k], sem.at[slot]),
            pltpu.make_async_copy(w_out_hbm.at[layer, cols, :], wo_buf.at[slot, k], sem.at[slot]),
        ]
    return copies


def _ffn_kernel(x_ref, g_ref, g2_ref, w_in_hbm, w_out_hbm, *refs, mode, layer):
    if mode == "emit_h":
        o_ref, hout_ref, h_ref, wg_buf, wu_buf, wo_buf, sem = refs
    else:
        o_ref, h_ref, wg_buf, wu_buf, wo_buf, sem = refs
    i = pl.program_id(0)
    n_tiles = pl.num_programs(0)
    copies = functools.partial(_ffn_group_copies, w_in_hbm, w_out_hbm, wg_buf, wu_buf, wo_buf,
                               sem, layer)
    n_row_chunks = TOKENS_PER_TILE // FFN_ROWS

    def start(group):
        for cp in copies(group):
            cp.start()

    def wait(group):
        for cp in copies(group):
            cp.wait()

    pl.when(i == 0)(lambda: start(0))

    for g in range(N_FFN_GROUPS):
        wait(g)
        if g + 1 < N_FFN_GROUPS:
            start(g + 1)
        else:
            pl.when(i + 1 < n_tiles)(lambda: start(0))
        if g == 0:
            for r in range(n_row_chunks):
                rows = pl.ds(r * FFN_ROWS, FFN_ROWS)
                h_ref[rows, :] = _rmsnorm(x_ref[rows, :], g_ref[...]).astype(BF16)
        slot = g % FFN_SLOTS
        for k, c in enumerate(_ffn_group_chunks(g)):
            wg = wg_buf[slot, k].astype(BF16)
            wu = wu_buf[slot, k].astype(BF16)
            wo = wo_buf[slot, k].astype(BF16)
            for r in range(n_row_chunks):
                rows = pl.ds(r * FFN_ROWS, FFN_ROWS)
                h = h_ref[rows, :]
                gate = _dot(h, wg)
                up = _dot(h, wu)
                act = (0.5 * (gate * _sigmoid(gate)) * up).astype(BF16)
                base = x_ref[rows, :] if c == 0 else o_ref[rows, :]
                o_ref[rows, :] = base + _dot(act, wo)

    if mode != "plain":
        for r in range(n_row_chunks):
            rows = pl.ds(r * FFN_ROWS, FFN_ROWS)
            y = _rmsnorm(o_ref[rows, :], g2_ref[...])
            if mode == "emit_h":
                hout_ref[rows, :] = y.astype(BF16)
            else:
                o_ref[rows, :] = y


def _ffn_call(x, norm, w_in, w_out, g2, layer, g2_layer, mode):
    n_tok = x.shape[0]
    nt = n_tok // TOKENS_PER_TILE
    tm = TOKENS_PER_TILE
    in_specs = [
        pl.BlockSpec((tm, D_MODEL), lambda i: (i, 0)),
        pl.BlockSpec((None, 1, D_MODEL), lambda i: (layer, 0, 0)),
        pl.BlockSpec((None, 1, D_MODEL), lambda i: (g2_layer, 0, 0)),
        pl.BlockSpec(memory_space=pl.ANY),
        pl.BlockSpec(memory_space=pl.ANY),
    ]
    out_shape = [jax.ShapeDtypeStruct((n_tok, D_MODEL), F32)]
    out_specs = [pl.BlockSpec((tm, D_MODEL), lambda i: (i, 0))]
    if mode == "emit_h":
        out_shape.append(jax.ShapeDtypeStruct((n_tok, D_MODEL), BF16))
        out_specs.append(pl.BlockSpec((tm, D_MODEL), lambda i: (i, 0)))
    return pl.pallas_call(
        functools.partial(_ffn_kernel, mode=mode, layer=layer),
        grid=(nt,),
        in_specs=in_specs,
        out_specs=out_specs,
        out_shape=out_shape,
        scratch_shapes=[
            pltpu.VMEM((tm, D_MODEL), BF16),
            pltpu.VMEM((FFN_SLOTS, FFN_GROUP, D_MODEL, FFN_COLS), F32),
            pltpu.VMEM((FFN_SLOTS, FFN_GROUP, D_MODEL, FFN_COLS), F32),
            pltpu.VMEM((FFN_SLOTS, FFN_GROUP, FFN_COLS, D_MODEL), F32),
            pltpu.SemaphoreType.DMA((FFN_SLOTS,)),
        ],
        compiler_params=pltpu.CompilerParams(
            dimension_semantics=("arbitrary",),
            vmem_limit_bytes=VMEM_LIMIT_BYTES),
        name=f"ffn_{mode}",
    )(x, norm, g2, w_in, w_out)


MIX_HALF = D_MODEL // 2
MIX_RING_COLS = 2 * D_MODEL
N_MIX_GROUPS = 6
_U, _V, _B1, _B2, _B3, _GA, _GB = (k * D_MODEL for k in range(7))


def _mixer_group_copies(w_in_hbm, wpa_hbm, wpb_hbm, wo_hbm, ring, sem, layer, group):
    slot = group % 2

    def part(p):
        return ring.at[slot, :, pl.ds(p * MIX_HALF, MIX_HALF)]

    def w_in_cols(start):
        return w_in_hbm.at[layer, :, pl.ds(start, MIX_HALF)]

    if group == 0:
        pairs = [(w_in_hbm.at[layer, :, pl.ds(_U, 2 * D_MODEL)], ring.at[slot])]
    elif group in (1, 2):
        off = (group - 1) * MIX_HALF
        pairs = [(w_in_cols(_B1 + off), part(0)), (w_in_cols(_B2 + off), part(1)),
                 (w_in_cols(_B3 + off), part(2))]
    elif group in (3, 4):
        off = (group - 3) * MIX_HALF
        pairs = [(w_in_cols(_GA + off), part(0)), (w_in_cols(_GB + off), part(1)),
                 (wpa_hbm.at[layer, :, pl.ds(off, MIX_HALF)], part(2)),
                 (wpb_hbm.at[layer, :, pl.ds(off, MIX_HALF)], part(3))]
    else:
        pairs = [(wo_hbm.at[layer], ring.at[slot, :, pl.ds(0, D_MODEL)])]
    return [pltpu.make_async_copy(src, dst, sem.at[slot]) for src, dst in pairs]


def _mixer_kernel(h_ref, x_hbm, w_in_hbm, wpa_hbm, wpb_hbm, wo_hbm, sgn_ref, sgw_ref, sgb_ref,
                  cw_ref, out_hbm, obuf, ring, vt_ref, s_ref, tb_ref, wsem, xsem, osem, *, layer):
    i = pl.program_id(0)
    n_tiles = pl.num_programs(0)
    copies = functools.partial(_mixer_group_copies, w_in_hbm, wpa_hbm, wpb_hbm, wo_hbm, ring,
                               wsem, layer)

    def start(group):
        for cp in copies(group):
            cp.start()

    def wait(group):
        for cp in copies(group):
            cp.wait()

    def tile_rows(t):
        return pl.ds(pl.multiple_of(t * SEQ, SEQ), SEQ)

    x_copy = pltpu.make_async_copy(x_hbm.at[tile_rows(i)], obuf, xsem)
    out_copy = pltpu.make_async_copy(obuf, out_hbm.at[tile_rows(i)], osem)

    def weight(slot, col, width=MIX_COLS):
        return ring[slot, :, col:col + width].astype(BF16)

    def proj_in(w):
        return _dot(h_ref[...], w)

    def contract(lhs_ref, slot, col):
        acc = None
        for k in range(N_MIX_BLOCKS):
            w = ring[slot, k * MIX_COLS:(k + 1) * MIX_COLS, col:col + MIX_COLS].astype(BF16)
            part = _dot(lhs_ref[k], w)
            acc = part if acc is None else acc + part
        return acc

    pl.when(i == 0)(lambda: start(0))

    wait(0)
    start(1)
    wv = weight(0, D_MODEL, D_MODEL)
    for r in range(SEQ // FFN_ROWS):
        v = _gelu_tanh(_dot(h_ref[r * FFN_ROWS:(r + 1) * FFN_ROWS, :], wv))
        vn = _rmsnorm(v, sgn_ref[...]).astype(BF16)
        for c in range(FFN_ROWS // CHUNK):
            chunk = r * (FFN_ROWS // CHUNK) + c
            for head in range(HEADS):
                vt_ref[head * CHUNK:(head + 1) * CHUNK, chunk * CHUNK:(chunk + 1) * CHUNK] = (
                    vn[c * CHUNK:(c + 1) * CHUNK, head * HEAD_DIM:(head + 1) * HEAD_DIM])
    for n in range(N_MIX_BLOCKS):
        u = _gelu_tanh(proj_in(weight(0, n * MIX_COLS)))
        for hh in range(HEADS_PER_BLOCK):
            head = n * HEADS_PER_BLOCK + hh
            gated = _dot(sgw_ref[head].astype(BF16),
                         vt_ref[head * CHUNK:(head + 1) * CHUNK, :]) + sgb_ref[head]
            for c in range(N_CHUNKS):
                s_ref[n, c * CHUNK:(c + 1) * CHUNK, hh * HEAD_DIM:(hh + 1) * HEAD_DIM] = (
                    u[c * CHUNK:(c + 1) * CHUNK, hh * HEAD_DIM:(hh + 1) * HEAD_DIM]
                    * gated[:, c * CHUNK:(c + 1) * CHUNK]).astype(BF16)

    for group in (1, 2):
        wait(group)
        if group == 1:
            pl.when(i > 0)(lambda: pltpu.make_async_copy(
                obuf, out_hbm.at[tile_rows(i - 1)], osem).wait())
            x_copy.start()
        start(group + 1)
        slot = group % 2
        for q in range(MIX_HALF // MIX_COLS):
            n = (group - 1) * (MIX_HALF // MIX_COLS) + q
            col = q * MIX_COLS
            t = proj_in(weight(slot, MIX_HALF + col)) * proj_in(weight(slot, 2 * MIX_HALF + col))
            row = lax.broadcasted_iota(jnp.int32, t.shape, 0)
            prev = jnp.where(row == 0, 0.0, pltpu.roll(t, 1, 0))
            nxt = jnp.where(row == SEQ - 1, 0.0, pltpu.roll(t, SEQ - 1, 0))
            conv = cw_ref[0, n] * prev + cw_ref[1, n] * t + cw_ref[2, n] * nxt
            tb_ref[n] = (proj_in(weight(slot, col)) * conv).astype(BF16)

    for group in (3, 4):
        wait(group)
        start(group + 1)
        slot = group % 2
        for q in range(MIX_HALF // MIX_COLS):
            n = (group - 3) * (MIX_HALF // MIX_COLS) + q
            col = q * MIX_COLS
            gate_a = _sigmoid(proj_in(weight(slot, col)))
            gate_b = _sigmoid(proj_in(weight(slot, MIX_HALF + col)))
            merged = (gate_a * contract(s_ref, slot, 2 * MIX_HALF + col)
                      + gate_b * contract(tb_ref, slot, 3 * MIX_HALF + col)).astype(BF16)
            for rh in range(2):
                vt_ref[:, rh * D_MODEL + n * MIX_COLS:rh * D_MODEL + (n + 1) * MIX_COLS] = (
                    merged[rh * (SEQ // 2):(rh + 1) * (SEQ // 2), :])

    wait(5)
    x_copy.wait()
    pl.when(i + 1 < n_tiles)(lambda: start(0))
    for n in range(N_MIX_BLOCKS):
        cols = slice(n * MIX_COLS, (n + 1) * MIX_COLS)
        for rh in range(2):
            acc = None
            for k in range(N_MIX_BLOCKS):
                w = ring[1, k * MIX_COLS:(k + 1) * MIX_COLS, cols].astype(BF16)
                lhs = vt_ref[:, rh * D_MODEL + k * MIX_COLS:rh * D_MODEL + (k + 1) * MIX_COLS]
                part = _dot(lhs, w)
                acc = part if acc is None else acc + part
            rows = slice(rh * (SEQ // 2), (rh + 1) * (SEQ // 2))
            obuf[rows, cols] = obuf[rows, cols] + acc
    out_copy.start()
    pl.when(i == n_tiles - 1)(lambda: out_copy.wait())


def _mixer_call(h, x, w_in, sgu_norm, sgu_w, sgu_b, conv_w, w_proj_a, w_proj_b, w_out, layer):
    n_tok = x.shape[0]
    nt = n_tok // SEQ
    nb = N_MIX_BLOCKS
    hbm = pl.BlockSpec(memory_space=pl.ANY)
    in_specs = [
        pl.BlockSpec((SEQ, D_MODEL), lambda i: (i, 0)),
        hbm, hbm, hbm, hbm, hbm,
        pl.BlockSpec((None, 1, D_MODEL), lambda i: (layer, 0, 0)),
        pl.BlockSpec((None, HEADS, CHUNK, CHUNK), lambda i: (layer, 0, 0, 0)),
        pl.BlockSpec((None, HEADS, CHUNK, 1), lambda i: (layer, 0, 0, 0)),
        pl.BlockSpec((None, 3, nb, 1, MIX_COLS), lambda i: (layer, 0, 0, 0, 0)),
    ]
    return pl.pallas_call(
        functools.partial(_mixer_kernel, layer=layer),
        grid=(nt,),
        in_specs=in_specs,
        out_specs=hbm,
        out_shape=jax.ShapeDtypeStruct((n_tok, D_MODEL), F32),
        scratch_shapes=[
            pltpu.VMEM((SEQ, D_MODEL), F32),
            pltpu.VMEM((2, D_MODEL, MIX_RING_COLS), F32),
            pltpu.VMEM((HEADS * CHUNK, SEQ), BF16),
            pltpu.VMEM((nb, SEQ, MIX_COLS), BF16),
            pltpu.VMEM((nb, SEQ, MIX_COLS), BF16),
            pltpu.SemaphoreType.DMA((2,)),
            pltpu.SemaphoreType.DMA,
            pltpu.SemaphoreType.DMA,
        ],
        compiler_params=pltpu.CompilerParams(
            dimension_semantics=("arbitrary",),
            vmem_limit_bytes=VMEM_LIMIT_BYTES),
        name="mixer",
    )(h, x, w_in, w_proj_a, w_proj_b, w_out, sgu_norm, sgu_w, sgu_b, conv_w)


def kernel(x, ffn1_norm, ffn1_w_in, ffn1_w_out, mix_norm, w_in, sgu_norm, sgu_w, sgu_b, conv_w,
           w_proj_a, w_proj_b, w_out, ffn2_norm, ffn2_w_in, ffn2_w_out, final_norm):
    batch, seq, d = x.shape
    depth = ffn1_norm.shape[0]
    assert (seq, d) == (SEQ, D_MODEL) and w_in.shape[-1] == N_IN and ffn1_w_out.shape[1] == D_FF
    xt = x.reshape(batch * seq, d)
    gains = lambda g: g.reshape(depth, 1, d)
    ffn1_g, mix_g, ffn2_g, sgu_g = gains(ffn1_norm), gains(mix_norm), gains(ffn2_norm), gains(sgu_norm)
    final_g = final_norm.reshape(1, 1, d)
    sgu_b4 = sgu_b.reshape(depth, HEADS, CHUNK, 1)
    conv_w5 = conv_w.reshape(depth, 3, N_MIX_BLOCKS, 1, MIX_COLS)
    for l in range(depth):
        xt, h = _ffn_call(xt, ffn1_g, ffn1_w_in, ffn1_w_out, mix_g, l, l, "emit_h")
        xt = _mixer_call(h, xt, w_in, sgu_g, sgu_w, sgu_b4, conv_w5, w_proj_a, w_proj_b, w_out, l)
        if l == depth - 1:
            (xt,) = _ffn_call(xt, ffn2_g, ffn2_w_in, ffn2_w_out, final_g, l, 0, "final")
        else:
            (xt,) = _ffn_call(xt, ffn2_g, ffn2_w_in, ffn2_w_out, ffn2_g, l, l, "plain")
    return xt.reshape(batch, seq, d)
```

```python
import functools

import numpy as np
import jax
import jax.numpy as jnp
from jax import lax
from jax.experimental import pallas as pl
from jax.experimental.pallas import tpu as pltpu

F32 = jnp.float32
BF16 = jnp.bfloat16
EPS = 1e-6

D_MODEL = 1024
SEQ = 2048
CHUNK = 128
HEADS = 8
HEAD_DIM = 128
D_FF = 2816
N_IN = 7 * D_MODEL

TOKENS_PER_TILE = SEQ
FFN_COLS = 256
FFN_ROWS = 1024
N_FFN_CHUNKS = D_FF // FFN_COLS
FFN_GROUP = 2
N_FFN_GROUPS = -(-N_FFN_CHUNKS // FFN_GROUP)
FFN_SLOTS = 2
assert N_FFN_GROUPS % FFN_SLOTS == 0
MIX_COLS = 256
MIX_ROWS = 512
N_MIX_BLOCKS = D_MODEL // MIX_COLS
N_CHUNKS = SEQ // CHUNK
HEADS_PER_BLOCK = MIX_COLS // HEAD_DIM
VMEM_LIMIT_BYTES = 60 * 1024 * 1024


def _rmsnorm(x, g):
    ms = jnp.mean(x * x, axis=-1, keepdims=True)
    return (x * lax.rsqrt(ms + EPS)) * g


def _gelu_tanh(x):
    c = np.float32(np.sqrt(2.0 / np.pi))
    return x * (0.5 * (1.0 + jnp.tanh(c * (x + 0.044715 * (x * x * x)))))


def _sigmoid(x):
    return 1.0 / (1.0 + jnp.exp(-x))


def _dot(a, b):
    return jnp.dot(a, b, preferred_element_type=F32)


def _ffn_group_chunks(group):
    return range(group * FFN_GROUP, min((group + 1) * FFN_GROUP, N_FFN_CHUNKS))


def _ffn_group_copies(w_in_hbm, w_out_hbm, wg_buf, wu_buf, wo_buf, sem, layer, group):
    slot = group % FFN_SLOTS
    copies = []
    for k, chunk in enumerate(_ffn_group_chunks(group)):
        cols = pl.ds(chunk * FFN_COLS, FFN_COLS)
        up_cols = pl.ds(D_FF + chunk * FFN_COLS, FFN_COLS)
        copies += [
            pltpu.make_async_copy(w_in_hbm.at[layer, :, cols], wg_buf.at[slot, k], sem.at[slot]),
            pltpu.make_async_copy(w_in_hbm.at[layer, :, up_cols], wu_buf.at[slot, k], sem.at[slot]),
            pltpu.make_async_copy(w_out_hbm.at[layer, cols, :], wo_buf.at[slot, k], sem.at[slot]),
        ]
    return copies


def _ffn_kernel(x_ref, g_ref, g2_ref, w_in_hbm, w_out_hbm, *refs, mode, layer):
    if mode == "emit_h":
        o_ref, hout_ref, h_ref, wg_buf, wu_buf, wo_buf, sem = refs
    else:
        o_ref, h_ref, wg_buf, wu_buf, wo_buf, sem = refs
    i = pl.program_id(0)
    n_tiles = pl.num_programs(0)
    copies = functools.partial(_ffn_group_copies, w_in_hbm, w_out_hbm, wg_buf, wu_buf, wo_buf,
                               sem, layer)
    n_row_chunks = TOKENS_PER_TILE // FFN_ROWS

    def start(group):
        for cp in copies(group):
            cp.start()

    def wait(group):
        for cp in copies(group):
            cp.wait()

    pl.when(i == 0)(lambda: start(0))

    for g in range(N_FFN_GROUPS):
        wait(g)
        if g + 1 < N_FFN_GROUPS:
            start(g + 1)
        else:
            pl.when(i + 1 < n_tiles)(lambda: start(0))
        if g == 0:
            for r in range(n_row_chunks):
                rows = pl.ds(r * FFN_ROWS, FFN_ROWS)
                h_ref[rows, :] = _rmsnorm(x_ref[rows, :], g_ref[...]).astype(BF16)
        slot = g % FFN_SLOTS
        for k, c in enumerate(_ffn_group_chunks(g)):
            wg = wg_buf[slot, k].astype(BF16)
            wu = wu_buf[slot, k].astype(BF16)
            wo = wo_buf[slot, k].astype(BF16)
            for r in range(n_row_chunks):
                rows = pl.ds(r * FFN_ROWS, FFN_ROWS)
                h = h_ref[rows, :]
                gate = _dot(h, wg)
                up = _dot(h, wu)
                act = (0.5 * (gate * _sigmoid(gate)) * up).astype(BF16)
                base = x_ref[rows, :] if c == 0 else o_ref[rows, :]
                o_ref[rows, :] = base + _dot(act, wo)

    if mode != "plain":
        for r in range(n_row_chunks):
            rows = pl.ds(r * FFN_ROWS, FFN_ROWS)
            y = _rmsnorm(o_ref[rows, :], g2_ref[...])
            if mode == "emit_h":
                hout_ref[rows, :] = y.astype(BF16)
            else:
                o_ref[rows, :] = y


def _ffn_call(x, norm, w_in, w_out, g2, layer, g2_layer, mode):
    n_tok = x.shape[0]
    nt = n_tok // TOKENS_PER_TILE
    tm = TOKENS_PER_TILE
    in_specs = [
        pl.BlockSpec((tm, D_MODEL), lambda i: (i, 0)),
        pl.BlockSpec((None, 1, D_MODEL), lambda i: (layer, 0, 0)),
        pl.BlockSpec((None, 1, D_MODEL), lambda i: (g2_layer, 0, 0)),
        pl.BlockSpec(memory_space=pl.ANY),
        pl.BlockSpec(memory_space=pl.ANY),
    ]
    out_shape = [jax.ShapeDtypeStruct((n_tok, D_MODEL), F32)]
    out_specs = [pl.BlockSpec((tm, D_MODEL), lambda i: (i, 0))]
    if mode == "emit_h":
        out_shape.append(jax.ShapeDtypeStruct((n_tok, D_MODEL), BF16))
        out_specs.append(pl.BlockSpec((tm, D_MODEL), lambda i: (i, 0)))
    return pl.pallas_call(
        functools.partial(_ffn_kernel, mode=mode, layer=layer),
        grid=(nt,),
        in_specs=in_specs,
        out_specs=out_specs,
        out_shape=out_shape,
        scratch_shapes=[
            pltpu.VMEM((tm, D_MODEL), BF16),
            pltpu.VMEM((FFN_SLOTS, FFN_GROUP, D_MODEL, FFN_COLS), F32),
            pltpu.VMEM((FFN_SLOTS, FFN_GROUP, D_MODEL, FFN_COLS), F32),
            pltpu.VMEM((FFN_SLOTS, FFN_GROUP, FFN_COLS, D_MODEL), F32),
            pltpu.SemaphoreType.DMA((FFN_SLOTS,)),
        ],
        compiler_params=pltpu.CompilerParams(
            dimension_semantics=("arbitrary",),
            vmem_limit_bytes=VMEM_LIMIT_BYTES),
        name=f"ffn_{mode}",
    )(x, norm, g2, w_in, w_out)


MIX_HALF = D_MODEL // 2
MIX_RING_COLS = 2 * D_MODEL
N_MIX_GROUPS = 6
_U, _V, _B1, _B2, _B3, _GA, _GB = (k * D_MODEL for k in range(7))


def _mixer_group_copies(w_in_hbm, wpa_hbm, wpb_hbm, wo_hbm, ring, sem, layer, group):
    slot = group % 2

    def part(p):
        return ring.at[slot, :, pl.ds(p * MIX_HALF, MIX_HALF)]

    def w_in_cols(start):
        return w_in_hbm.at[layer, :, pl.ds(start, MIX_HALF)]

    if group == 0:
        pairs = [(w_in_hbm.at[layer, :, pl.ds(_U, 2 * D_MODEL)], ring.at[slot])]
    elif group in (1, 2):
        off = (group - 1) * MIX_HALF
        pairs = [(w_in_cols(_B1 + off), part(0)), (w_in_cols(_B2 + off), part(1)),
                 (w_in_cols(_B3 + off), part(2))]
    elif group in (3, 4):
        off = (group - 3) * MIX_HALF
        pairs = [(w_in_cols(_GA + off), part(0)), (w_in_cols(_GB + off), part(1)),
                 (wpa_hbm.at[layer, :, pl.ds(off, MIX_HALF)], part(2)),
                 (wpb_hbm.at[layer, :, pl.ds(off, MIX_HALF)], part(3))]
    else:
        pairs = [(wo_hbm.at[layer], ring.at[slot, :, pl.ds(0, D_MODEL)])]
    return [pltpu.make_async_copy(src, dst, sem.at[slot]) for src, dst in pairs]


def _mixer_kernel(h_ref, x_hbm, w_in_hbm, wpa_hbm, wpb_hbm, wo_hbm, sgn_ref, sgw_ref, sgb_ref,
                  cw_ref, out_hbm, obuf, ring, vt_ref, s_ref, tb_ref, wsem, xsem, osem, *, layer):
    i = pl.program_id(0)
    n_tiles = pl.num_programs(0)
    copies = functools.partial(_mixer_group_copies, w_in_hbm, wpa_hbm, wpb_hbm, wo_hbm, ring,
                               wsem, layer)

    def start(group):
        for cp in copies(group):
            cp.start()

    def wait(group):
        for cp in copies(group):
            cp.wait()

    def tile_rows(t):
        return pl.ds(pl.multiple_of(t * SEQ, SEQ), SEQ)

    x_copy = pltpu.make_async_copy(x_hbm.at[tile_rows(i)], obuf, xsem)
    out_copy = pltpu.make_async_copy(obuf, out_hbm.at[tile_rows(i)], osem)

    def weight(slot, col, width=MIX_COLS):
        return ring[slot, :, col:col + width].astype(BF16)

    def proj_in(w):
        return _dot(h_ref[...], w)

    def contract(lhs_ref, slot, col):
        return _dot(lhs_ref[...], weight(slot, col))

    pl.when(i == 0)(lambda: start(0))

    wait(0)
    start(1)
    wv = weight(0, D_MODEL, D_MODEL)
    for r in range(SEQ // MIX_ROWS):
        v = _gelu_tanh(_dot(h_ref[r * MIX_ROWS:(r + 1) * MIX_ROWS, :], wv))
        vn = _rmsnorm(v, sgn_ref[...]).astype(BF16)
        for c in range(MIX_ROWS // CHUNK):
            chunk = r * (MIX_ROWS // CHUNK) + c
            for head in range(HEADS):
                vt_ref[head * CHUNK:(head + 1) * CHUNK, chunk * CHUNK:(chunk + 1) * CHUNK] = (
                    vn[c * CHUNK:(c + 1) * CHUNK, head * HEAD_DIM:(head + 1) * HEAD_DIM])
    zero_block = jnp.zeros((CHUNK, CHUNK), BF16)
    for n in range(N_MIX_BLOCKS):
        u = _gelu_tanh(proj_in(weight(0, n * MIX_COLS)))
        h0, h1 = n * HEADS_PER_BLOCK, n * HEADS_PER_BLOCK + 1
        w_pair = jnp.concatenate([
            jnp.concatenate([sgw_ref[h0].astype(BF16), zero_block], axis=1),
            jnp.concatenate([zero_block, sgw_ref[h1].astype(BF16)], axis=1)], axis=0)
        bias = jnp.concatenate([sgb_ref[h0], sgb_ref[h1]], axis=0)
        gated = _dot(w_pair, vt_ref[h0 * CHUNK:(h1 + 1) * CHUNK, :]) + bias
        for hh in range(HEADS_PER_BLOCK):
            for c in range(N_CHUNKS):
                s_ref[c * CHUNK:(c + 1) * CHUNK, (h0 + hh) * HEAD_DIM:(h0 + hh + 1) * HEAD_DIM] = (
                    u[c * CHUNK:(c + 1) * CHUNK, hh * HEAD_DIM:(hh + 1) * HEAD_DIM]
                    * gated[hh * CHUNK:(hh + 1) * CHUNK, c * CHUNK:(c + 1) * CHUNK]).astype(BF16)

    for group in (1, 2):
        wait(group)
        if group == 1:
            pl.when(i > 0)(lambda: pltpu.make_async_copy(
                obuf, out_hbm.at[tile_rows(i - 1)], osem).wait())
            x_copy.start()
        start(group + 1)
        slot = group % 2
        blocks = range(MIX_HALF // MIX_COLS)
        convs = []
        for q in blocks:
            n = (group - 1) * len(blocks) + q
            col = q * MIX_COLS
            t = proj_in(weight(slot, MIX_HALF + col)) * proj_in(weight(slot, 2 * MIX_HALF + col))
            row = lax.broadcasted_iota(jnp.int32, t.shape, 0)
            prev = jnp.where(row == 0, 0.0, pltpu.roll(t, 1, 0))
            nxt = jnp.where(row == SEQ - 1, 0.0, pltpu.roll(t, SEQ - 1, 0))
            convs.append(cw_ref[0, n] * prev + cw_ref[1, n] * t + cw_ref[2, n] * nxt)
        for q in blocks:
            n = (group - 1) * len(blocks) + q
            tb_ref[:, n * MIX_COLS:(n + 1) * MIX_COLS] = (
                proj_in(weight(slot, q * MIX_COLS)) * convs[q]).astype(BF16)

    for group in (3, 4):
        wait(group)
        start(group + 1)
        slot = group % 2
        blocks = range(MIX_HALF // MIX_COLS)
        gates = [(_sigmoid(proj_in(weight(slot, q * MIX_COLS))),
                  _sigmoid(proj_in(weight(slot, MIX_HALF + q * MIX_COLS)))) for q in blocks]
        for q in blocks:
            n = (group - 3) * len(blocks) + q
            col = q * MIX_COLS
            merged = (gates[q][0] * contract(s_ref, slot, 2 * MIX_HALF + col)
                      + gates[q][1] * contract(tb_ref, slot, 3 * MIX_HALF + col)).astype(BF16)
            for rh in range(2):
                vt_ref[:, rh * D_MODEL + n * MIX_COLS:rh * D_MODEL + (n + 1) * MIX_COLS] = (
                    merged[rh * (SEQ // 2):(rh + 1) * (SEQ // 2), :])

    wait(5)
    x_copy.wait()
    pl.when(i + 1 < n_tiles)(lambda: start(0))
    for n in range(N_MIX_BLOCKS):
        cols = slice(n * MIX_COLS, (n + 1) * MIX_COLS)
        w = weight(1, n * MIX_COLS)
        for rh in range(2):
            rows = slice(rh * (SEQ // 2), (rh + 1) * (SEQ // 2))
            obuf[rows, cols] = obuf[rows, cols] + _dot(vt_ref[:, rh * D_MODEL:(rh + 1) * D_MODEL], w)
    out_copy.start()
    pl.when(i == n_tiles - 1)(lambda: out_copy.wait())


def _mixer_call(h, x, w_in, sgu_norm, sgu_w, sgu_b, conv_w, w_proj_a, w_proj_b, w_out, layer):
    n_tok = x.shape[0]
    nt = n_tok // SEQ
    nb = N_MIX_BLOCKS
    hbm = pl.BlockSpec(memory_space=pl.ANY)
    in_specs = [
        pl.BlockSpec((SEQ, D_MODEL), lambda i: (i, 0)),
        hbm, hbm, hbm, hbm, hbm,
        pl.BlockSpec((None, 1, D_MODEL), lambda i: (layer, 0, 0)),
        pl.BlockSpec((None, HEADS, CHUNK, CHUNK), lambda i: (layer, 0, 0, 0)),
        pl.BlockSpec((None, HEADS, CHUNK, 1), lambda i: (layer, 0, 0, 0)),
        pl.BlockSpec((None, 3, nb, 1, MIX_COLS), lambda i: (layer, 0, 0, 0, 0)),
    ]
    return pl.pallas_call(
        functools.partial(_mixer_kernel, layer=layer),
        grid=(nt,),
        in_specs=in_specs,
        out_specs=hbm,
        out_shape=jax.ShapeDtypeStruct((n_tok, D_MODEL), F32),
        scratch_shapes=[
            pltpu.VMEM((SEQ, D_MODEL), F32),
            pltpu.VMEM((2, D_MODEL, MIX_RING_COLS), F32),
            pltpu.VMEM((HEADS * CHUNK, SEQ), BF16),
            pltpu.VMEM((SEQ, D_MODEL), BF16),
            pltpu.VMEM((SEQ, D_MODEL), BF16),
            pltpu.SemaphoreType.DMA((2,)),
            pltpu.SemaphoreType.DMA,
            pltpu.SemaphoreType.DMA,
        ],
        compiler_params=pltpu.CompilerParams(
            dimension_semantics=("arbitrary",),
            vmem_limit_bytes=VMEM_LIMIT_BYTES),
        name="mixer",
    )(h, x, w_in, w_proj_a, w_proj_b, w_out, sgu_norm, sgu_w, sgu_b, conv_w)


def kernel(x, ffn1_norm, ffn1_w_in, ffn1_w_out, mix_norm, w_in, sgu_norm, sgu_w, sgu_b, conv_w,
           w_proj_a, w_proj_b, w_out, ffn2_norm, ffn2_w_in, ffn2_w_out, final_norm):
    batch, seq, d = x.shape
    depth = ffn1_norm.shape[0]
    assert (seq, d) == (SEQ, D_MODEL) and w_in.shape[-1] == N_IN and ffn1_w_out.shape[1] == D_FF
    xt = x.reshape(batch * seq, d)
    gains = lambda g: g.reshape(depth, 1, d)
    ffn1_g, mix_g, ffn2_g, sgu_g = gains(ffn1_norm), gains(mix_norm), gains(ffn2_norm), gains(sgu_norm)
    final_g = final_norm.reshape(1, 1, d)
    sgu_b4 = sgu_b.reshape(depth, HEADS, CHUNK, 1)
    conv_w5 = conv_w.reshape(depth, 3, N_MIX_BLOCKS, 1, MIX_COLS)
    for l in range(depth):
        xt, h = _ffn_call(xt, ffn1_g, ffn1_w_in, ffn1_w_out, mix_g, l, l, "emit_h")
        xt = _mixer_call(h, xt, w_in, sgu_g, sgu_w, sgu_b4, conv_w5, w_proj_a, w_proj_b, w_out, l)
        if l == depth - 1:
            (xt,) = _ffn_call(xt, ffn2_g, ffn2_w_in, ffn2_w_out, final_g, l, 0, "final")
        else:
            (xt,) = _ffn_call(xt, ffn2_g, ffn2_w_in, ffn2_w_out, ffn2_g, l, l, "plain")
    return xt.reshape(batch, seq, d)
```

```python
import functools

import numpy as np
import jax
import jax.numpy as jnp
from jax import lax
from jax.experimental import pallas as pl
from jax.experimental.pallas import tpu as pltpu

F32 = jnp.float32
BF16 = jnp.bfloat16
EPS = 1e-6

D_MODEL = 1024
SEQ = 2048
CHUNK = 128
HEADS = 8
HEAD_DIM = 128
D_FF = 2816
N_IN = 7 * D_MODEL

TOKENS_PER_TILE = SEQ
FFN_COLS = 256
FFN_ROWS = 1024
N_FFN_CHUNKS = D_FF // FFN_COLS
FFN_GROUP = 2
N_FFN_GROUPS = -(-N_FFN_CHUNKS // FFN_GROUP)
FFN_SLOTS = 2
assert N_FFN_GROUPS % FFN_SLOTS == 0
MIX_COLS = 256
MIX_ROWS = 512
CONV_PAD = 8
N_MIX_BLOCKS = D_MODEL // MIX_COLS
N_CHUNKS = SEQ // CHUNK
HEADS_PER_BLOCK = MIX_COLS // HEAD_DIM
VMEM_LIMIT_BYTES = 60 * 1024 * 1024


def _rmsnorm(x, g):
    ms = jnp.mean(x * x, axis=-1, keepdims=True)
    return (x * lax.rsqrt(ms + EPS)) * g


def _gelu_tanh(x):
    c = np.float32(np.sqrt(2.0 / np.pi))
    return x * (0.5 * (1.0 + jnp.tanh(c * (x + 0.044715 * (x * x * x)))))


def _sigmoid(x):
    return 1.0 / (1.0 + jnp.exp(-x))


def _dot(a, b):
    return jnp.dot(a, b, preferred_element_type=F32)


def _ffn_group_chunks(group):
    return range(group * FFN_GROUP, min((group + 1) * FFN_GROUP, N_FFN_CHUNKS))


def _ffn_group_copies(w_in_hbm, w_out_hbm, wg_buf, wu_buf, wo_buf, sem, layer, group):
    slot = group % FFN_SLOTS
    copies = []
    for k, chunk in enumerate(_ffn_group_chunks(group)):
        cols = pl.ds(chunk * FFN_COLS, FFN_COLS)
        up_cols = pl.ds(D_FF + chunk * FFN_COLS, FFN_COLS)
        copies += [
            pltpu.make_async_copy(w_in_hbm.at[layer, :, cols], wg_buf.at[slot, k], sem.at[slot]),
            pltpu.make_async_copy(w_in_hbm.at[layer, :, up_cols], wu_buf.at[slot, k], sem.at[slot]),
            pltpu.make_async_copy(w_out_hbm.at[layer, cols, :], wo_buf.at[slot, k], sem.at[slot]),
        ]
    return copies


def _ffn_kernel(x_ref, g_ref, g2_ref, w_in_hbm, w_out_hbm, *refs, mode, layer):
    if mode == "emit_h":
        o_ref, hout_ref, h_ref, wg_buf, wu_buf, wo_buf, sem = refs
    else:
        o_ref, h_ref, wg_buf, wu_buf, wo_buf, sem = refs
    i = pl.program_id(0)
    n_tiles = pl.num_programs(0)
    copies = functools.partial(_ffn_group_copies, w_in_hbm, w_out_hbm, wg_buf, wu_buf, wo_buf,
                               sem, layer)
    n_row_chunks = TOKENS_PER_TILE // FFN_ROWS

    def start(group):
        for cp in copies(group):
            cp.start()

    def wait(group):
        for cp in copies(group):
            cp.wait()

    pl.when(i == 0)(lambda: start(0))

    for g in range(N_FFN_GROUPS):
        wait(g)
        if g + 1 < N_FFN_GROUPS:
            start(g + 1)
        else:
            pl.when(i + 1 < n_tiles)(lambda: start(0))
        if g == 0:
            for r in range(n_row_chunks):
                rows = pl.ds(r * FFN_ROWS, FFN_ROWS)
                h_ref[rows, :] = _rmsnorm(x_ref[rows, :], g_ref[...]).astype(BF16)
        slot = g % FFN_SLOTS
        for k, c in enumerate(_ffn_group_chunks(g)):
            wg = wg_buf[slot, k].astype(BF16)
            wu = wu_buf[slot, k].astype(BF16)
            wo = wo_buf[slot, k].astype(BF16)
            for r in range(n_row_chunks):
                rows = pl.ds(r * FFN_ROWS, FFN_ROWS)
                h = h_ref[rows, :]
                gate = _dot(h, wg)
                up = _dot(h, wu)
                act = (0.5 * (gate * _sigmoid(gate)) * up).astype(BF16)
                base = x_ref[rows, :] if c == 0 else o_ref[rows, :]
                o_ref[rows, :] = base + _dot(act, wo)

    if mode != "plain":
        for r in range(n_row_chunks):
            rows = pl.ds(r * FFN_ROWS, FFN_ROWS)
            y = _rmsnorm(o_ref[rows, :], g2_ref[...])
            if mode == "emit_h":
                hout_ref[rows, :] = y.astype(BF16)
            else:
                o_ref[rows, :] = y


def _ffn_call(x, norm, w_in, w_out, g2, layer, g2_layer, mode):
    n_tok = x.shape[0]
    nt = n_tok // TOKENS_PER_TILE
    tm = TOKENS_PER_TILE
    in_specs = [
        pl.BlockSpec((tm, D_MODEL), lambda i: (i, 0)),
        pl.BlockSpec((None, 1, D_MODEL), lambda i: (layer, 0, 0)),
        pl.BlockSpec((None, 1, D_MODEL), lambda i: (g2_layer, 0, 0)),
        pl.BlockSpec(memory_space=pl.ANY),
        pl.BlockSpec(memory_space=pl.ANY),
    ]
    out_shape = [jax.ShapeDtypeStruct((n_tok, D_MODEL), F32)]
    out_specs = [pl.BlockSpec((tm, D_MODEL), lambda i: (i, 0))]
    if mode == "emit_h":
        out_shape.append(jax.ShapeDtypeStruct((n_tok, D_MODEL), BF16))
        out_specs.append(pl.BlockSpec((tm, D_MODEL), lambda i: (i, 0)))
    return pl.pallas_call(
        functools.partial(_ffn_kernel, mode=mode, layer=layer),
        grid=(nt,),
        in_specs=in_specs,
        out_specs=out_specs,
        out_shape=out_shape,
        scratch_shapes=[
            pltpu.VMEM((tm, D_MODEL), BF16),
            pltpu.VMEM((FFN_SLOTS, FFN_GROUP, D_MODEL, FFN_COLS), F32),
            pltpu.VMEM((FFN_SLOTS, FFN_GROUP, D_MODEL, FFN_COLS), F32),
            pltpu.VMEM((FFN_SLOTS, FFN_GROUP, FFN_COLS, D_MODEL), F32),
            pltpu.SemaphoreType.DMA((FFN_SLOTS,)),
        ],
        compiler_params=pltpu.CompilerParams(
            dimension_semantics=("arbitrary",),
            vmem_limit_bytes=VMEM_LIMIT_BYTES),
        name=f"ffn_{mode}",
    )(x, norm, g2, w_in, w_out)


MIX_HALF = D_MODEL // 2
MIX_RING_COLS = 2 * D_MODEL
N_MIX_GROUPS = 6
_U, _V, _B1, _B2, _B3, _GA, _GB = (k * D_MODEL for k in range(7))


def _mixer_group_copies(w_in_hbm, wpa_hbm, wpb_hbm, wo_hbm, ring, sem, layer, group):
    slot = group % 2

    def part(p):
        return ring.at[slot, :, pl.ds(p * MIX_HALF, MIX_HALF)]

    def w_in_cols(start):
        return w_in_hbm.at[layer, :, pl.ds(start, MIX_HALF)]

    if group == 0:
        pairs = [(w_in_hbm.at[layer, :, pl.ds(_U, 2 * D_MODEL)], ring.at[slot])]
    elif group in (1, 2):
        off = (group - 1) * MIX_HALF
        pairs = [(w_in_cols(_B1 + off), part(0)), (w_in_cols(_B2 + off), part(1)),
                 (w_in_cols(_B3 + off), part(2))]
    elif group in (3, 4):
        off = (group - 3) * MIX_HALF
        pairs = [(w_in_cols(_GA + off), part(0)), (w_in_cols(_GB + off), part(1)),
                 (wpa_hbm.at[layer, :, pl.ds(off, MIX_HALF)], part(2)),
                 (wpb_hbm.at[layer, :, pl.ds(off, MIX_HALF)], part(3))]
    else:
        pairs = [(wo_hbm.at[layer], ring.at[slot, :, pl.ds(0, D_MODEL)])]
    return [pltpu.make_async_copy(src, dst, sem.at[slot]) for src, dst in pairs]


def _mixer_kernel(h_ref, x_hbm, w_in_hbm, wpa_hbm, wpb_hbm, wo_hbm, sgn_ref, sgw_ref, sgb_ref,
                  cw_ref, out_hbm, obuf, ring, vt_ref, s_ref, tb_ref, g_ref, t_ref, wsem, xsem, osem,
                  *, layer):
    i = pl.program_id(0)
    n_tiles = pl.num_programs(0)
    copies = functools.partial(_mixer_group_copies, w_in_hbm, wpa_hbm, wpb_hbm, wo_hbm, ring,
                               wsem, layer)

    def start(group):
        for cp in copies(group):
            cp.start()

    def wait(group):
        for cp in copies(group):
            cp.wait()

    def tile_rows(t):
        return pl.ds(pl.multiple_of(t * SEQ, SEQ), SEQ)

    x_copy = pltpu.make_async_copy(x_hbm.at[tile_rows(i)], obuf, xsem)
    out_copy = pltpu.make_async_copy(obuf, out_hbm.at[tile_rows(i)], osem)

    def weight(slot, col, width=MIX_COLS):
        return ring[slot, :, col:col + width].astype(BF16)

    row_chunks = [slice(r * MIX_ROWS, (r + 1) * MIX_ROWS) for r in range(SEQ // MIX_ROWS)]

    pl.when(i == 0)(lambda: start(0))

    wait(0)
    start(1)
    wv = weight(0, D_MODEL, D_MODEL)
    for r, rows in enumerate(row_chunks):
        v = _gelu_tanh(_dot(h_ref[rows, :], wv))
        vn = _rmsnorm(v, sgn_ref[...]).astype(BF16)
        for c in range(MIX_ROWS // CHUNK):
            chunk = r * (MIX_ROWS // CHUNK) + c
            for head in range(HEADS):
                vt_ref[head * CHUNK:(head + 1) * CHUNK, chunk * CHUNK:(chunk + 1) * CHUNK] = (
                    vn[c * CHUNK:(c + 1) * CHUNK, head * HEAD_DIM:(head + 1) * HEAD_DIM])
    zero_block = jnp.zeros((CHUNK, CHUNK), BF16)
    for n in range(N_MIX_BLOCKS):
        h0, h1 = n * HEADS_PER_BLOCK, n * HEADS_PER_BLOCK + 1
        w_pair = jnp.concatenate([
            jnp.concatenate([sgw_ref[h0].astype(BF16), zero_block], axis=1),
            jnp.concatenate([zero_block, sgw_ref[h1].astype(BF16)], axis=1)], axis=0)
        bias = jnp.concatenate([sgb_ref[h0], sgb_ref[h1]], axis=0)
        g_ref[...] = _dot(w_pair, vt_ref[h0 * CHUNK:(h1 + 1) * CHUNK, :]) + bias
        wu = weight(0, n * MIX_COLS)
        for r, rows in enumerate(row_chunks):
            u = _gelu_tanh(_dot(h_ref[rows, :], wu))
            for hh in range(HEADS_PER_BLOCK):
                for c in range(MIX_ROWS // CHUNK):
                    chunk = r * (MIX_ROWS // CHUNK) + c
                    s_ref[chunk * CHUNK:(chunk + 1) * CHUNK,
                          (h0 + hh) * HEAD_DIM:(h0 + hh + 1) * HEAD_DIM] = (
                        u[c * CHUNK:(c + 1) * CHUNK, hh * HEAD_DIM:(hh + 1) * HEAD_DIM]
                        * g_ref[hh * CHUNK:(hh + 1) * CHUNK, chunk * CHUNK:(chunk + 1) * CHUNK]
                    ).astype(BF16)

    t_ref[0:CONV_PAD, :] = jnp.zeros((CONV_PAD, MIX_COLS), F32)
    t_ref[CONV_PAD + SEQ:, :] = jnp.zeros((CONV_PAD, MIX_COLS), F32)
    for group in (1, 2):
        wait(group)
        if group == 1:
            pl.when(i > 0)(lambda: pltpu.make_async_copy(
                obuf, out_hbm.at[tile_rows(i - 1)], osem).wait())
            x_copy.start()
        start(group + 1)
        slot = group % 2
        for q in range(MIX_HALF // MIX_COLS):
            n = (group - 1) * (MIX_HALF // MIX_COLS) + q
            col = q * MIX_COLS
            w_gate, w_c, w_x = (weight(slot, p * MIX_HALF + col) for p in range(3))
            for rows in row_chunks:
                h = h_ref[rows, :]
                t_ref[CONV_PAD + rows.start:CONV_PAD + rows.stop, :] = _dot(h, w_c) * _dot(h, w_x)
            for rows in row_chunks:
                lo = CONV_PAD + rows.start
                conv = (cw_ref[0, n] * t_ref[lo - 1:lo - 1 + MIX_ROWS, :]
                        + cw_ref[1, n] * t_ref[lo:lo + MIX_ROWS, :]
                        + cw_ref[2, n] * t_ref[lo + 1:lo + 1 + MIX_ROWS, :])
                tb_ref[rows, n * MIX_COLS:(n + 1) * MIX_COLS] = (
                    _dot(h_ref[rows, :], w_gate) * conv).astype(BF16)

    for group in (3, 4):
        wait(group)
        start(group + 1)
        slot = group % 2
        for q in range(MIX_HALF // MIX_COLS):
            n = (group - 3) * (MIX_HALF // MIX_COLS) + q
            col = q * MIX_COLS
            w_ga, w_gb, w_pa, w_pb = (weight(slot, p * MIX_HALF + col) for p in range(4))
            for r, rows in enumerate(row_chunks):
                h = h_ref[rows, :]
                merged = _sigmoid(_dot(h, w_ga)) * _dot(s_ref[rows, :], w_pa)
                merged = merged + _sigmoid(_dot(h, w_gb)) * _dot(tb_ref[rows, :], w_pb)
                half, local = divmod(rows.start, SEQ // 2)
                vt_ref[local:local + MIX_ROWS,
                       half * D_MODEL + n * MIX_COLS:half * D_MODEL + (n + 1) * MIX_COLS] = (
                    merged.astype(BF16))

    wait(5)
    x_copy.wait()
    pl.when(i + 1 < n_tiles)(lambda: start(0))
    for n in range(N_MIX_BLOCKS):
        cols = slice(n * MIX_COLS, (n + 1) * MIX_COLS)
        w = weight(1, n * MIX_COLS)
        for rows in row_chunks:
            half, local = divmod(rows.start, SEQ // 2)
            lhs = vt_ref[local:local + MIX_ROWS, half * D_MODEL:(half + 1) * D_MODEL]
            obuf[rows, cols] = obuf[rows, cols] + _dot(lhs, w)
    out_copy.start()
    pl.when(i == n_tiles - 1)(lambda: out_copy.wait())


def _mixer_call(h, x, w_in, sgu_norm, sgu_w, sgu_b, conv_w, w_proj_a, w_proj_b, w_out, layer):
    n_tok = x.shape[0]
    nt = n_tok // SEQ
    nb = N_MIX_BLOCKS
    hbm = pl.BlockSpec(memory_space=pl.ANY)
    in_specs = [
        pl.BlockSpec((SEQ, D_MODEL), lambda i: (i, 0)),
        hbm, hbm, hbm, hbm, hbm,
        pl.BlockSpec((None, 1, D_MODEL), lambda i: (layer, 0, 0)),
        pl.BlockSpec((None, HEADS, CHUNK, CHUNK), lambda i: (layer, 0, 0, 0)),
        pl.BlockSpec((None, HEADS, CHUNK, 1), lambda i: (layer, 0, 0, 0)),
        pl.BlockSpec((None, 3, nb, 1, MIX_COLS), lambda i: (layer, 0, 0, 0, 0)),
    ]
    return pl.pallas_call(
        functools.partial(_mixer_kernel, layer=layer),
        grid=(nt,),
        in_specs=in_specs,
        out_specs=hbm,
        out_shape=jax.ShapeDtypeStruct((n_tok, D_MODEL), F32),
        scratch_shapes=[
            pltpu.VMEM((SEQ, D_MODEL), F32),
            pltpu.VMEM((2, D_MODEL, MIX_RING_COLS), F32),
            pltpu.VMEM((HEADS * CHUNK, SEQ), BF16),
            pltpu.VMEM((SEQ, D_MODEL), BF16),
            pltpu.VMEM((SEQ, D_MODEL), BF16),
            pltpu.VMEM((HEADS_PER_BLOCK * CHUNK, SEQ), F32),
            pltpu.VMEM((SEQ + 2 * CONV_PAD, MIX_COLS), F32),
            pltpu.SemaphoreType.DMA((2,)),
            pltpu.SemaphoreType.DMA,
            pltpu.SemaphoreType.DMA,
        ],
        compiler_params=pltpu.CompilerParams(
            dimension_semantics=("arbitrary",),
            vmem_limit_bytes=VMEM_LIMIT_BYTES),
        name="mixer",
    )(h, x, w_in, w_proj_a, w_proj_b, w_out, sgu_norm, sgu_w, sgu_b, conv_w)


def kernel(x, ffn1_norm, ffn1_w_in, ffn1_w_out, mix_norm, w_in, sgu_norm, sgu_w, sgu_b, conv_w,
           w_proj_a, w_proj_b, w_out, ffn2_norm, ffn2_w_in, ffn2_w_out, final_norm):
    batch, seq, d = x.shape
    depth = ffn1_norm.shape[0]
    assert (seq, d) == (SEQ, D_MODEL) and w_in.shape[-1] == N_IN and ffn1_w_out.shape[1] == D_FF
    xt = x.reshape(batch * seq, d)
    gains = lambda g: g.reshape(depth, 1, d)
    ffn1_g, mix_g, ffn2_g, sgu_g = gains(ffn1_norm), gains(mix_norm), gains(ffn2_norm), gains(sgu_norm)
    final_g = final_norm.reshape(1, 1, d)
    sgu_b4 = sgu_b.reshape(depth, HEADS, CHUNK, 1)
    conv_w5 = conv_w.reshape(depth, 3, N_MIX_BLOCKS, 1, MIX_COLS)
    for l in range(depth):
        xt, h = _ffn_call(xt, ffn1_g, ffn1_w_in, ffn1_w_out, mix_g, l, l, "emit_h")
        xt = _mixer_call(h, xt, w_in, sgu_g, sgu_w, sgu_b4, conv_w5, w_proj_a, w_proj_b, w_out, l)
        if l == depth - 1:
            (xt,) = _ffn_call(xt, ffn2_g, ffn2_w_in, ffn2_w_out, final_g, l, 0, "final")
        else:
            (xt,) = _ffn_call(xt, ffn2_g, ffn2_w_in, ffn2_w_out, ffn2_g, l, l, "plain")
    return xt.reshape(batch, seq, d)
```

```python
import functools

import numpy as np
import jax
import jax.numpy as jnp
from jax import lax
from jax.experimental import pallas as pl
from jax.experimental.pallas import tpu as pltpu

F32 = jnp.float32
BF16 = jnp.bfloat16
EPS = 1e-6

D_MODEL = 1024
SEQ = 2048
CHUNK = 128
HEADS = 8
HEAD_DIM = 128
D_FF = 2816
N_IN = 7 * D_MODEL

TOKENS_PER_TILE = SEQ
FFN_COLS = 256
FFN_ROWS = 1024
N_FFN_CHUNKS = D_FF // FFN_COLS
FFN_GROUP = 2
N_FFN_GROUPS = -(-N_FFN_CHUNKS // FFN_GROUP)
FFN_SLOTS = 2
assert N_FFN_GROUPS % FFN_SLOTS == 0
MIX_COLS = 256
MIX_ROWS = 1024
CONV_PAD = 8
N_MIX_BLOCKS = D_MODEL // MIX_COLS
N_CHUNKS = SEQ // CHUNK
HEADS_PER_BLOCK = MIX_COLS // HEAD_DIM
VMEM_LIMIT_BYTES = 60 * 1024 * 1024


def _rmsnorm(x, g):
    ms = jnp.mean(x * x, axis=-1, keepdims=True)
    return (x * lax.rsqrt(ms + EPS)) * g


def _gelu_tanh(x):
    c = np.float32(np.sqrt(2.0 / np.pi))
    return x * (0.5 * (1.0 + jnp.tanh(c * (x + 0.044715 * (x * x * x)))))


def _sigmoid(x):
    return 0.5 * jnp.tanh(0.5 * x) + 0.5


def _dot(a, b):
    return jnp.dot(a, b, preferred_element_type=F32)


def _ffn_group_chunks(group):
    return range(group * FFN_GROUP, min((group + 1) * FFN_GROUP, N_FFN_CHUNKS))


def _ffn_group_copies(w_in_hbm, w_out_hbm, wg_buf, wu_buf, wo_buf, sem, layer, group):
    slot = group % FFN_SLOTS
    copies = []
    for k, chunk in enumerate(_ffn_group_chunks(group)):
        cols = pl.ds(chunk * FFN_COLS, FFN_COLS)
        up_cols = pl.ds(D_FF + chunk * FFN_COLS, FFN_COLS)
        copies += [
            pltpu.make_async_copy(w_in_hbm.at[layer, :, cols], wg_buf.at[slot, k], sem.at[slot]),
            pltpu.make_async_copy(w_in_hbm.at[layer, :, up_cols], wu_buf.at[slot, k], sem.at[slot]),
            pltpu.make_async_copy(w_out_hbm.at[layer, cols, :], wo_buf.at[slot, k], sem.at[slot]),
        ]
    return copies


def _ffn_kernel(x_ref, g_ref, g2_ref, w_in_hbm, w_out_hbm, *refs, mode, layer):
    if mode == "emit_h":
        o_ref, hout_ref, h_ref, wg_buf, wu_buf, wo_buf, sem = refs
    else:
        o_ref, h_ref, wg_buf, wu_buf, wo_buf, sem = refs
    i = pl.program_id(0)
    n_tiles = pl.num_programs(0)
    copies = functools.partial(_ffn_group_copies, w_in_hbm, w_out_hbm, wg_buf, wu_buf, wo_buf,
                               sem, layer)
    n_row_chunks = TOKENS_PER_TILE // FFN_ROWS

    def start(group):
        for cp in copies(group):
            cp.start()

    def wait(group):
        for cp in copies(group):
            cp.wait()

    pl.when(i == 0)(lambda: start(0))

    for g in range(N_FFN_GROUPS):
        wait(g)
        if g + 1 < N_FFN_GROUPS:
            start(g + 1)
        else:
            pl.when(i + 1 < n_tiles)(lambda: start(0))
        if g == 0:
            for r in range(n_row_chunks):
                rows = pl.ds(r * FFN_ROWS, FFN_ROWS)
                h_ref[rows, :] = _rmsnorm(x_ref[rows, :], g_ref[...]).astype(BF16)
        slot = g % FFN_SLOTS
        for k, c in enumerate(_ffn_group_chunks(g)):
            wg = wg_buf[slot, k].astype(BF16)
            wu = wu_buf[slot, k].astype(BF16)
            wo = wo_buf[slot, k].astype(BF16)
            for r in range(n_row_chunks):
                rows = pl.ds(r * FFN_ROWS, FFN_ROWS)
                h = h_ref[rows, :]
                gate = _dot(h, wg)
                up = _dot(h, wu)
                act = (0.5 * (gate * _sigmoid(gate)) * up).astype(BF16)
                base = x_ref[rows, :] if c == 0 else o_ref[rows, :]
                o_ref[rows, :] = base + _dot(act, wo)

    if mode != "plain":
        for r in range(n_row_chunks):
            rows = pl.ds(r * FFN_ROWS, FFN_ROWS)
            y = _rmsnorm(o_ref[rows, :], g2_ref[...])
            if mode == "emit_h":
                hout_ref[rows, :] = y.astype(BF16)
            else:
                o_ref[rows, :] = y


def _ffn_call(x, norm, w_in, w_out, g2, layer, g2_layer, mode):
    n_tok = x.shape[0]
    nt = n_tok // TOKENS_PER_TILE
    tm = TOKENS_PER_TILE
    in_specs = [
        pl.BlockSpec((tm, D_MODEL), lambda i: (i, 0)),
        pl.BlockSpec((None, 1, D_MODEL), lambda i: (layer, 0, 0)),
        pl.BlockSpec((None, 1, D_MODEL), lambda i: (g2_layer, 0, 0)),
        pl.BlockSpec(memory_space=pl.ANY),
        pl.BlockSpec(memory_space=pl.ANY),
    ]
    out_shape = [jax.ShapeDtypeStruct((n_tok, D_MODEL), F32)]
    out_specs = [pl.BlockSpec((tm, D_MODEL), lambda i: (i, 0))]
    if mode == "emit_h":
        out_shape.append(jax.ShapeDtypeStruct((n_tok, D_MODEL), BF16))
        out_specs.append(pl.BlockSpec((tm, D_MODEL), lambda i: (i, 0)))
    return pl.pallas_call(
        functools.partial(_ffn_kernel, mode=mode, layer=layer),
        grid=(nt,),
        in_specs=in_specs,
        out_specs=out_specs,
        out_shape=out_shape,
        scratch_shapes=[
            pltpu.VMEM((tm, D_MODEL), BF16),
            pltpu.VMEM((FFN_SLOTS, FFN_GROUP, D_MODEL, FFN_COLS), F32),
            pltpu.VMEM((FFN_SLOTS, FFN_GROUP, D_MODEL, FFN_COLS), F32),
            pltpu.VMEM((FFN_SLOTS, FFN_GROUP, FFN_COLS, D_MODEL), F32),
            pltpu.SemaphoreType.DMA((FFN_SLOTS,)),
        ],
        compiler_params=pltpu.CompilerParams(
            dimension_semantics=("arbitrary",),
            vmem_limit_bytes=VMEM_LIMIT_BYTES),
        name=f"ffn_{mode}",
    )(x, norm, g2, w_in, w_out)


MIX_HALF = D_MODEL // 2
MIX_RING_COLS = 2 * D_MODEL
N_MIX_GROUPS = 6
_U, _V, _B1, _B2, _B3, _GA, _GB = (k * D_MODEL for k in range(7))


def _mixer_group_copies(w_in_hbm, wpa_hbm, wpb_hbm, wo_hbm, ring, sem, layer, group):
    slot = group % 2

    def part(p):
        return ring.at[slot, :, pl.ds(p * MIX_HALF, MIX_HALF)]

    def w_in_cols(start):
        return w_in_hbm.at[layer, :, pl.ds(start, MIX_HALF)]

    if group == 0:
        pairs = [(w_in_hbm.at[layer, :, pl.ds(_U, 2 * D_MODEL)], ring.at[slot])]
    elif group in (1, 2):
        off = (group - 1) * MIX_HALF
        pairs = [(w_in_cols(_B1 + off), part(0)), (w_in_cols(_B2 + off), part(1)),
                 (w_in_cols(_B3 + off), part(2))]
    elif group in (3, 4):
        off = (group - 3) * MIX_HALF
        pairs = [(w_in_cols(_GA + off), part(0)), (w_in_cols(_GB + off), part(1)),
                 (wpa_hbm.at[layer, :, pl.ds(off, MIX_HALF)], part(2)),
                 (wpb_hbm.at[layer, :, pl.ds(off, MIX_HALF)], part(3))]
    else:
        pairs = [(wo_hbm.at[layer], ring.at[slot, :, pl.ds(0, D_MODEL)])]
    return [pltpu.make_async_copy(src, dst, sem.at[slot]) for src, dst in pairs]


def _mixer_kernel(h_ref, x_hbm, w_in_hbm, wpa_hbm, wpb_hbm, wo_hbm, sgn_ref, sgw_ref, sgb_ref,
                  cw_ref, out_hbm, obuf, ring, vt_ref, s_ref, tb_ref, g_ref, t_ref, v_ref,
                  wsem, xsem, osem, *, layer):
    i = pl.program_id(0)
    n_tiles = pl.num_programs(0)
    copies = functools.partial(_mixer_group_copies, w_in_hbm, wpa_hbm, wpb_hbm, wo_hbm, ring,
                               wsem, layer)

    def start(group):
        for cp in copies(group):
            cp.start()

    def wait(group):
        for cp in copies(group):
            cp.wait()

    def tile_rows(t):
        return pl.ds(pl.multiple_of(t * SEQ, SEQ), SEQ)

    x_copy = pltpu.make_async_copy(x_hbm.at[tile_rows(i)], obuf, xsem)
    out_copy = pltpu.make_async_copy(obuf, out_hbm.at[tile_rows(i)], osem)

    def weight(slot, col, width=MIX_COLS):
        return ring[slot, :, col:col + width].astype(BF16)

    row_chunks = [slice(r * MIX_ROWS, (r + 1) * MIX_ROWS) for r in range(SEQ // MIX_ROWS)]

    pl.when(i == 0)(lambda: start(0))

    wait(0)
    start(1)
    wv_blocks = [weight(0, _V + n * MIX_COLS) for n in range(N_MIX_BLOCKS)]
    for r, rows in enumerate(row_chunks):
        for n in range(N_MIX_BLOCKS):
            v_ref[:, n * MIX_COLS:(n + 1) * MIX_COLS] = _gelu_tanh(_dot(h_ref[rows, :], wv_blocks[n]))
        vn = _rmsnorm(v_ref[...], sgn_ref[...]).astype(BF16)
        for c in range(MIX_ROWS // CHUNK):
            chunk = r * (MIX_ROWS // CHUNK) + c
            for head in range(HEADS):
                vt_ref[head * CHUNK:(head + 1) * CHUNK, chunk * CHUNK:(chunk + 1) * CHUNK] = (
                    vn[c * CHUNK:(c + 1) * CHUNK, head * HEAD_DIM:(head + 1) * HEAD_DIM])
    zero_block = jnp.zeros((CHUNK, CHUNK), BF16)
    for n in range(N_MIX_BLOCKS):
        h0, h1 = n * HEADS_PER_BLOCK, n * HEADS_PER_BLOCK + 1
        w_pair = jnp.concatenate([
            jnp.concatenate([sgw_ref[h0].astype(BF16), zero_block], axis=1),
            jnp.concatenate([zero_block, sgw_ref[h1].astype(BF16)], axis=1)], axis=0)
        bias = jnp.concatenate([sgb_ref[h0], sgb_ref[h1]], axis=0)
        g_ref[...] = _dot(w_pair, vt_ref[h0 * CHUNK:(h1 + 1) * CHUNK, :]) + bias
        wu = weight(0, n * MIX_COLS)
        for r, rows in enumerate(row_chunks):
            u = _gelu_tanh(_dot(h_ref[rows, :], wu))
            for hh in range(HEADS_PER_BLOCK):
                for c in range(MIX_ROWS // CHUNK):
                    chunk = r * (MIX_ROWS // CHUNK) + c
                    s_ref[chunk * CHUNK:(chunk + 1) * CHUNK,
                          (h0 + hh) * HEAD_DIM:(h0 + hh + 1) * HEAD_DIM] = (
                        u[c * CHUNK:(c + 1) * CHUNK, hh * HEAD_DIM:(hh + 1) * HEAD_DIM]
                        * g_ref[hh * CHUNK:(hh + 1) * CHUNK, chunk * CHUNK:(chunk + 1) * CHUNK]
                    ).astype(BF16)

    t_ref[0:CONV_PAD, :] = jnp.zeros((CONV_PAD, MIX_COLS), F32)
    t_ref[CONV_PAD + SEQ:, :] = jnp.zeros((CONV_PAD, MIX_COLS), F32)
    for group in (1, 2):
        wait(group)
        if group == 1:
            pl.when(i > 0)(lambda: pltpu.make_async_copy(
                obuf, out_hbm.at[tile_rows(i - 1)], osem).wait())
            x_copy.start()
        start(group + 1)
        slot = group % 2
        for q in range(MIX_HALF // MIX_COLS):
            n = (group - 1) * (MIX_HALF // MIX_COLS) + q
            col = q * MIX_COLS
            w_gate, w_c, w_x = (weight(slot, p * MIX_HALF + col) for p in range(3))
            for rows in row_chunks:
                h = h_ref[rows, :]
                t_ref[CONV_PAD + rows.start:CONV_PAD + rows.stop, :] = _dot(h, w_c) * _dot(h, w_x)
            for rows in row_chunks:
                lo = CONV_PAD + rows.start
                conv = (cw_ref[0, n] * t_ref[lo - 1:lo - 1 + MIX_ROWS, :]
                        + cw_ref[1, n] * t_ref[lo:lo + MIX_ROWS, :]
                        + cw_ref[2, n] * t_ref[lo + 1:lo + 1 + MIX_ROWS, :])
                tb_ref[rows, n * MIX_COLS:(n + 1) * MIX_COLS] = (
                    _dot(h_ref[rows, :], w_gate) * conv).astype(BF16)

    for group in (3, 4):
        wait(group)
        start(group + 1)
        slot = group % 2
        for q in range(MIX_HALF // MIX_COLS):
            n = (group - 3) * (MIX_HALF // MIX_COLS) + q
            col = q * MIX_COLS
            w_ga, w_gb, w_pa, w_pb = (weight(slot, p * MIX_HALF + col) for p in range(4))
            for r, rows in enumerate(row_chunks):
                h = h_ref[rows, :]
                merged = _sigmoid(_dot(h, w_ga)) * _dot(s_ref[rows, :], w_pa)
                merged = merged + _sigmoid(_dot(h, w_gb)) * _dot(tb_ref[rows, :], w_pb)
                half, local = divmod(rows.start, SEQ // 2)
                vt_ref[local:local + MIX_ROWS,
                       half * D_MODEL + n * MIX_COLS:half * D_MODEL + (n + 1) * MIX_COLS] = (
                    merged.astype(BF16))

    wait(5)
    x_copy.wait()
    pl.when(i + 1 < n_tiles)(lambda: start(0))
    for n in range(N_MIX_BLOCKS):
        cols = slice(n * MIX_COLS, (n + 1) * MIX_COLS)
        w = weight(1, n * MIX_COLS)
        for rows in row_chunks:
            half, local = divmod(rows.start, SEQ // 2)
            lhs = vt_ref[local:local + MIX_ROWS, half * D_MODEL:(half + 1) * D_MODEL]
            obuf[rows, cols] = obuf[rows, cols] + _dot(lhs, w)
    out_copy.start()
    pl.when(i == n_tiles - 1)(lambda: out_copy.wait())


def _mixer_call(h, x, w_in, sgu_norm, sgu_w, sgu_b, conv_w, w_proj_a, w_proj_b, w_out, layer):
    n_tok = x.shape[0]
    nt = n_tok // SEQ
    nb = N_MIX_BLOCKS
    hbm = pl.BlockSpec(memory_space=pl.ANY)
    in_specs = [
        pl.BlockSpec((SEQ, D_MODEL), lambda i: (i, 0)),
        hbm, hbm, hbm, hbm, hbm,
        pl.BlockSpec((None, 1, D_MODEL), lambda i: (layer, 0, 0)),
        pl.BlockSpec((None, HEADS, CHUNK, CHUNK), lambda i: (layer, 0, 0, 0)),
        pl.BlockSpec((None, HEADS, CHUNK, 1), lambda i: (layer, 0, 0, 0)),
        pl.BlockSpec((None, 3, nb, 1, MIX_COLS), lambda i: (layer, 0, 0, 0, 0)),
    ]
    return pl.pallas_call(
        functools.partial(_mixer_kernel, layer=layer),
        grid=(nt,),
        in_specs=in_specs,
        out_specs=hbm,
        out_shape=jax.ShapeDtypeStruct((n_tok, D_MODEL), F32),
        scratch_shapes=[
            pltpu.VMEM((SEQ, D_MODEL), F32),
            pltpu.VMEM((2, D_MODEL, MIX_RING_COLS), F32),
            pltpu.VMEM((HEADS * CHUNK, SEQ), BF16),
            pltpu.VMEM((SEQ, D_MODEL), BF16),
            pltpu.VMEM((SEQ, D_MODEL), BF16),
            pltpu.VMEM((HEADS_PER_BLOCK * CHUNK, SEQ), F32),
            pltpu.VMEM((SEQ + 2 * CONV_PAD, MIX_COLS), F32),
            pltpu.VMEM((MIX_ROWS, D_MODEL), F32),
            pltpu.SemaphoreType.DMA((2,)),
            pltpu.SemaphoreType.DMA,
            pltpu.SemaphoreType.DMA,
        ],
        compiler_params=pltpu.CompilerParams(
            dimension_semantics=("arbitrary",),
            vmem_limit_bytes=VMEM_LIMIT_BYTES),
        name="mixer",
    )(h, x, w_in, w_proj_a, w_proj_b, w_out, sgu_norm, sgu_w, sgu_b, conv_w)


def kernel(x, ffn1_norm, ffn1_w_in, ffn1_w_out, mix_norm, w_in, sgu_norm, sgu_w, sgu_b, conv_w,
           w_proj_a, w_proj_b, w_out, ffn2_norm, ffn2_w_in, ffn2_w_out, final_norm):
    batch, seq, d = x.shape
    depth = ffn1_norm.shape[0]
    assert (seq, d) == (SEQ, D_MODEL) and w_in.shape[-1] == N_IN and ffn1_w_out.shape[1] == D_FF
    xt = x.reshape(batch * seq, d)
    gains = lambda g: g.reshape(depth, 1, d)
    ffn1_g, mix_g, ffn2_g, sgu_g = gains(ffn1_norm), gains(mix_norm), gains(ffn2_norm), gains(sgu_norm)
    final_g = final_norm.reshape(1, 1, d)
    sgu_b4 = sgu_b.reshape(depth, HEADS, CHUNK, 1)
    conv_w5 = conv_w.reshape(depth, 3, N_MIX_BLOCKS, 1, MIX_COLS)
    for l in range(depth):
        xt, h = _ffn_call(xt, ffn1_g, ffn1_w_in, ffn1_w_out, mix_g, l, l, "emit_h")
        xt = _mixer_call(h, xt, w_in, sgu_g, sgu_w, sgu_b4, conv_w5, w_proj_a, w_proj_b, w_out, l)
        if l == depth - 1:
            (xt,) = _ffn_call(xt, ffn2_g, ffn2_w_in, ffn2_w_out, final_g, l, 0, "final")
        else:
            (xt,) = _ffn_call(xt, ffn2_g, ffn2_w_in, ffn2_w_out, ffn2_g, l, l, "plain")
    return xt.reshape(batch, seq, d)
```

```python
import functools

import numpy as np
import jax
import jax.numpy as jnp
from jax import lax
from jax.experimental import pallas as pl
from jax.experimental.pallas import tpu as pltpu

F32 = jnp.float32
BF16 = jnp.bfloat16
EPS = 1e-6

D_MODEL = 1024
SEQ = 2048
CHUNK = 128
HEADS = 8
HEAD_DIM = 128
D_FF = 2816
N_IN = 7 * D_MODEL

TOKENS_PER_TILE = SEQ
FFN_COLS = 256
FFN_ROWS = 1024
N_FFN_CHUNKS = D_FF // FFN_COLS
FFN_GROUP = 3
N_FFN_GROUPS = -(-N_FFN_CHUNKS // FFN_GROUP)
FFN_SLOTS = 2
assert N_FFN_GROUPS % FFN_SLOTS == 0
MIX_COLS = 256
MIX_ROWS = 1024
CONV_PAD = 8
N_MIX_BLOCKS = D_MODEL // MIX_COLS
N_CHUNKS = SEQ // CHUNK
HEADS_PER_BLOCK = MIX_COLS // HEAD_DIM
VMEM_LIMIT_BYTES = 60 * 1024 * 1024


def _rmsnorm(x, g):
    ms = jnp.mean(x * x, axis=-1, keepdims=True)
    return (x * lax.rsqrt(ms + EPS)) * g


def _gelu_tanh(x):
    c = np.float32(np.sqrt(2.0 / np.pi))
    return x * (0.5 * (1.0 + jnp.tanh(c * (x + 0.044715 * (x * x * x)))))


def _sigmoid(x):
    return 0.5 * jnp.tanh(0.5 * x) + 0.5


def _dot(a, b):
    return jnp.dot(a, b, preferred_element_type=F32)


def _ffn_group_chunks(group):
    return range(group * FFN_GROUP, min((group + 1) * FFN_GROUP, N_FFN_CHUNKS))


def _ffn_group_copies(w_in_hbm, w_out_hbm, wg_buf, wu_buf, wo_buf, sem, layer, group):
    slot = group % FFN_SLOTS
    copies = []
    for k, chunk in enumerate(_ffn_group_chunks(group)):
        cols = pl.ds(chunk * FFN_COLS, FFN_COLS)
        up_cols = pl.ds(D_FF + chunk * FFN_COLS, FFN_COLS)
        copies += [
            pltpu.make_async_copy(w_in_hbm.at[layer, :, cols], wg_buf.at[slot, k], sem.at[slot]),
            pltpu.make_async_copy(w_in_hbm.at[layer, :, up_cols], wu_buf.at[slot, k], sem.at[slot]),
            pltpu.make_async_copy(w_out_hbm.at[layer, cols, :], wo_buf.at[slot, k], sem.at[slot]),
        ]
    return copies


def _ffn_kernel(g_ref, g2_ref, x_hbm, w_in_hbm, w_out_hbm, out_hbm, *refs, mode, layer):
    if mode == "emit_h":
        hout_ref, obuf, h_ref, wg_buf, wu_buf, wo_buf, sem, xsem, osem = refs
    else:
        obuf, h_ref, wg_buf, wu_buf, wo_buf, sem, xsem, osem = refs
    i = pl.program_id(0)
    n_tiles = pl.num_programs(0)
    copies = functools.partial(_ffn_group_copies, w_in_hbm, w_out_hbm, wg_buf, wu_buf, wo_buf,
                               sem, layer)
    n_row_chunks = TOKENS_PER_TILE // FFN_ROWS
    tile = obuf.at[i % 2]

    def start(group):
        for cp in copies(group):
            cp.start()

    def wait(group):
        for cp in copies(group):
            cp.wait()

    def tile_rows(t):
        return pl.ds(pl.multiple_of(t * TOKENS_PER_TILE, TOKENS_PER_TILE), TOKENS_PER_TILE)

    def x_copy(t):
        return pltpu.make_async_copy(x_hbm.at[tile_rows(t)], obuf.at[t % 2], xsem)

    def out_copy(t):
        return pltpu.make_async_copy(obuf.at[t % 2], out_hbm.at[tile_rows(t)], osem)

    @pl.when(i == 0)
    def _():
        x_copy(i).start()
        start(0)

    for g in range(N_FFN_GROUPS):
        wait(g)
        if g == 0:
            x_copy(i).wait()
        if g == 1:
            pl.when(i > 0)(lambda: out_copy(i - 1).wait())
            pl.when(i + 1 < n_tiles)(lambda: x_copy(i + 1).start())
        if g + 1 < N_FFN_GROUPS:
            start(g + 1)
        else:
            pl.when(i + 1 < n_tiles)(lambda: start(0))
        if g == 0:
            for r in range(n_row_chunks):
                rows = pl.ds(r * FFN_ROWS, FFN_ROWS)
                h_ref[rows, :] = _rmsnorm(tile[rows, :], g_ref[...]).astype(BF16)
        slot = g % FFN_SLOTS
        for k, c in enumerate(_ffn_group_chunks(g)):
            wg = wg_buf[slot, k].astype(BF16)
            wu = wu_buf[slot, k].astype(BF16)
            wo = wo_buf[slot, k].astype(BF16)
            for r in range(n_row_chunks):
                rows = pl.ds(r * FFN_ROWS, FFN_ROWS)
                h = h_ref[rows, :]
                gate = _dot(h, wg)
                up = _dot(h, wu)
                act = (0.5 * (gate * _sigmoid(gate)) * up).astype(BF16)
                tile[rows, :] = tile[rows, :] + _dot(act, wo)

    if mode != "plain":
        for r in range(n_row_chunks):
            rows = pl.ds(r * FFN_ROWS, FFN_ROWS)
            y = _rmsnorm(tile[rows, :], g2_ref[...])
            if mode == "emit_h":
                hout_ref[rows, :] = y.astype(BF16)
            else:
                tile[rows, :] = y
    out_copy(i).start()
    pl.when(i == n_tiles - 1)(lambda: out_copy(i).wait())


def _ffn_call(x, norm, w_in, w_out, g2, layer, g2_layer, mode):
    n_tok = x.shape[0]
    nt = n_tok // TOKENS_PER_TILE
    tm = TOKENS_PER_TILE
    hbm = pl.BlockSpec(memory_space=pl.ANY)
    in_specs = [
        pl.BlockSpec((None, 1, D_MODEL), lambda i: (layer, 0, 0)),
        pl.BlockSpec((None, 1, D_MODEL), lambda i: (g2_layer, 0, 0)),
        hbm, hbm, hbm,
    ]
    out_shape = [jax.ShapeDtypeStruct((n_tok, D_MODEL), F32)]
    out_specs = [hbm]
    if mode == "emit_h":
        out_shape.append(jax.ShapeDtypeStruct((n_tok, D_MODEL), BF16))
        out_specs.append(pl.BlockSpec((tm, D_MODEL), lambda i: (i, 0)))
    return pl.pallas_call(
        functools.partial(_ffn_kernel, mode=mode, layer=layer),
        grid=(nt,),
        in_specs=in_specs,
        out_specs=out_specs,
        out_shape=out_shape,
        scratch_shapes=[
            pltpu.VMEM((2, tm, D_MODEL), F32),
            pltpu.VMEM((tm, D_MODEL), BF16),
            pltpu.VMEM((FFN_SLOTS, FFN_GROUP, D_MODEL, FFN_COLS), F32),
            pltpu.VMEM((FFN_SLOTS, FFN_GROUP, D_MODEL, FFN_COLS), F32),
            pltpu.VMEM((FFN_SLOTS, FFN_GROUP, FFN_COLS, D_MODEL), F32),
            pltpu.SemaphoreType.DMA((FFN_SLOTS,)),
            pltpu.SemaphoreType.DMA,
            pltpu.SemaphoreType.DMA,
        ],
        compiler_params=pltpu.CompilerParams(
            dimension_semantics=("arbitrary",),
            vmem_limit_bytes=VMEM_LIMIT_BYTES),
        name=f"ffn_{mode}",
    )(norm, g2, x, w_in, w_out)


MIX_HALF = D_MODEL // 2
MIX_RING_COLS = 2 * D_MODEL
N_MIX_GROUPS = 6
_U, _V, _B1, _B2, _B3, _GA, _GB = (k * D_MODEL for k in range(7))


def _mixer_group_copies(w_in_hbm, wpa_hbm, wpb_hbm, wo_hbm, ring, sem, layer, group):
    slot = group % 2

    def part(p):
        return ring.at[slot, :, pl.ds(p * MIX_HALF, MIX_HALF)]

    def w_in_cols(start):
        return w_in_hbm.at[layer, :, pl.ds(start, MIX_HALF)]

    if group == 0:
        pairs = [(w_in_hbm.at[layer, :, pl.ds(_U, 2 * D_MODEL)], ring.at[slot])]
    elif group in (1, 2):
        off = (group - 1) * MIX_HALF
        pairs = [(w_in_cols(_B1 + off), part(0)), (w_in_cols(_B2 + off), part(1)),
                 (w_in_cols(_B3 + off), part(2))]
    elif group in (3, 4):
        off = (group - 3) * MIX_HALF
        pairs = [(w_in_cols(_GA + off), part(0)), (w_in_cols(_GB + off), part(1)),
                 (wpa_hbm.at[layer, :, pl.ds(off, MIX_HALF)], part(2)),
                 (wpb_hbm.at[layer, :, pl.ds(off, MIX_HALF)], part(3))]
    else:
        pairs = [(wo_hbm.at[layer], ring.at[slot, :, pl.ds(0, D_MODEL)])]
    return [pltpu.make_async_copy(src, dst, sem.at[slot]) for src, dst in pairs]


def _mixer_kernel(h_ref, x_hbm, w_in_hbm, wpa_hbm, wpb_hbm, wo_hbm, sgn_ref, sgw_ref, sgb_ref,
                  cw_ref, out_hbm, obuf, ring, vt_ref, s_ref, tb_ref, g_ref, t_ref, v_ref,
                  wsem, xsem, osem, *, layer):
    i = pl.program_id(0)
    n_tiles = pl.num_programs(0)
    copies = functools.partial(_mixer_group_copies, w_in_hbm, wpa_hbm, wpb_hbm, wo_hbm, ring,
                               wsem, layer)

    def start(group):
        for cp in copies(group):
            cp.start()

    def wait(group):
        for cp in copies(group):
            cp.wait()

    def tile_rows(t):
        return pl.ds(pl.multiple_of(t * SEQ, SEQ), SEQ)

    x_copy = pltpu.make_async_copy(x_hbm.at[tile_rows(i)], obuf, xsem)
    out_copy = pltpu.make_async_copy(obuf, out_hbm.at[tile_rows(i)], osem)

    def weight(slot, col, width=MIX_COLS):
        return ring[slot, :, col:col + width].astype(BF16)

    row_chunks = [slice(r * MIX_ROWS, (r + 1) * MIX_ROWS) for r in range(SEQ // MIX_ROWS)]

    pl.when(i == 0)(lambda: start(0))

    wait(0)
    start(1)
    wv_blocks = [weight(0, _V + n * MIX_COLS) for n in range(N_MIX_BLOCKS)]
    for r, rows in enumerate(row_chunks):
        for n in range(N_MIX_BLOCKS):
            v_ref[:, n * MIX_COLS:(n + 1) * MIX_COLS] = _gelu_tanh(_dot(h_ref[rows, :], wv_blocks[n]))
        vn = _rmsnorm(v_ref[...], sgn_ref[...]).astype(BF16)
        for c in range(MIX_ROWS // CHUNK):
            chunk = r * (MIX_ROWS // CHUNK) + c
            for head in range(HEADS):
                vt_ref[head * CHUNK:(head + 1) * CHUNK, chunk * CHUNK:(chunk + 1) * CHUNK] = (
                    vn[c * CHUNK:(c + 1) * CHUNK, head * HEAD_DIM:(head + 1) * HEAD_DIM])
    zero_block = jnp.zeros((CHUNK, CHUNK), BF16)
    for n in range(N_MIX_BLOCKS):
        h0, h1 = n * HEADS_PER_BLOCK, n * HEADS_PER_BLOCK + 1
        w_pair = jnp.concatenate([
            jnp.concatenate([sgw_ref[h0].astype(BF16), zero_block], axis=1),
            jnp.concatenate([zero_block, sgw_ref[h1].astype(BF16)], axis=1)], axis=0)
        bias = jnp.concatenate([sgb_ref[h0], sgb_ref[h1]], axis=0)
        g_ref[...] = _dot(w_pair, vt_ref[h0 * CHUNK:(h1 + 1) * CHUNK, :]) + bias
        wu = weight(0, n * MIX_COLS)
        for r, rows in enumerate(row_chunks):
            u = _gelu_tanh(_dot(h_ref[rows, :], wu))
            for hh in range(HEADS_PER_BLOCK):
                for c in range(MIX_ROWS // CHUNK):
                    chunk = r * (MIX_ROWS // CHUNK) + c
                    s_ref[chunk * CHUNK:(chunk + 1) * CHUNK,
                          (h0 + hh) * HEAD_DIM:(h0 + hh + 1) * HEAD_DIM] = (
                        u[c * CHUNK:(c + 1) * CHUNK, hh * HEAD_DIM:(hh + 1) * HEAD_DIM]
                        * g_ref[hh * CHUNK:(hh + 1) * CHUNK, chunk * CHUNK:(chunk + 1) * CHUNK]
                    ).astype(BF16)

    t_ref[0:CONV_PAD, :] = jnp.zeros((CONV_PAD, MIX_COLS), F32)
    t_ref[CONV_PAD + SEQ:, :] = jnp.zeros((CONV_PAD, MIX_COLS), F32)
    for group in (1, 2):
        wait(group)
        if group == 1:
            pl.when(i > 0)(lambda: pltpu.make_async_copy(
                obuf, out_hbm.at[tile_rows(i - 1)], osem).wait())
            x_copy.start()
        start(group + 1)
        slot = group % 2
        for q in range(MIX_HALF // MIX_COLS):
            n = (group - 1) * (MIX_HALF // MIX_COLS) + q
            col = q * MIX_COLS
            w_gate, w_c, w_x = (weight(slot, p * MIX_HALF + col) for p in range(3))
            for rows in row_chunks:
                h = h_ref[rows, :]
                t_ref[CONV_PAD + rows.start:CONV_PAD + rows.stop, :] = _dot(h, w_c) * _dot(h, w_x)
            for rows in row_chunks:
                lo = CONV_PAD + rows.start
                conv = (cw_ref[0, n] * t_ref[lo - 1:lo - 1 + MIX_ROWS, :]
                        + cw_ref[1, n] * t_ref[lo:lo + MIX_ROWS, :]
                        + cw_ref[2, n] * t_ref[lo + 1:lo + 1 + MIX_ROWS, :])
                tb_ref[rows, n * MIX_COLS:(n + 1) * MIX_COLS] = (
                    _dot(h_ref[rows, :], w_gate) * conv).astype(BF16)

    for group in (3, 4):
        wait(group)
        start(group + 1)
        slot = group % 2
        for q in range(MIX_HALF // MIX_COLS):
            n = (group - 3) * (MIX_HALF // MIX_COLS) + q
            col = q * MIX_COLS
            w_ga, w_gb, w_pa, w_pb = (weight(slot, p * MIX_HALF + col) for p in range(4))
            for r, rows in enumerate(row_chunks):
                h = h_ref[rows, :]
                merged = _sigmoid(_dot(h, w_ga)) * _dot(s_ref[rows, :], w_pa)
                merged = merged + _sigmoid(_dot(h, w_gb)) * _dot(tb_ref[rows, :], w_pb)
                half, local = divmod(rows.start, SEQ // 2)
                vt_ref[local:local + MIX_ROWS,
                       half * D_MODEL + n * MIX_COLS:half * D_MODEL + (n + 1) * MIX_COLS] = (
                    merged.astype(BF16))

    wait(5)
    x_copy.wait()
    pl.when(i + 1 < n_tiles)(lambda: start(0))
    for n in range(N_MIX_BLOCKS):
        cols = slice(n * MIX_COLS, (n + 1) * MIX_COLS)
        w = weight(1, n * MIX_COLS)
        for rows in row_chunks:
            half, local = divmod(rows.start, SEQ // 2)
            lhs = vt_ref[local:local + MIX_ROWS, half * D_MODEL:(half + 1) * D_MODEL]
            obuf[rows, cols] = obuf[rows, cols] + _dot(lhs, w)
    out_copy.start()
    pl.when(i == n_tiles - 1)(lambda: out_copy.wait())


def _mixer_call(h, x, w_in, sgu_norm, sgu_w, sgu_b, conv_w, w_proj_a, w_proj_b, w_out, layer):
    n_tok = x.shape[0]
    nt = n_tok // SEQ
    nb = N_MIX_BLOCKS
    hbm = pl.BlockSpec(memory_space=pl.ANY)
    in_specs = [
        pl.BlockSpec((SEQ, D_MODEL), lambda i: (i, 0)),
        hbm, hbm, hbm, hbm, hbm,
        pl.BlockSpec((None, 1, D_MODEL), lambda i: (layer, 0, 0)),
        pl.BlockSpec((None, HEADS, CHUNK, CHUNK), lambda i: (layer, 0, 0, 0)),
        pl.BlockSpec((None, HEADS, CHUNK, 1), lambda i: (layer, 0, 0, 0)),
        pl.BlockSpec((None, 3, nb, 1, MIX_COLS), lambda i: (layer, 0, 0, 0, 0)),
    ]
    return pl.pallas_call(
        functools.partial(_mixer_kernel, layer=layer),
        grid=(nt,),
        in_specs=in_specs,
        out_specs=hbm,
        out_shape=jax.ShapeDtypeStruct((n_tok, D_MODEL), F32),
        scratch_shapes=[
            pltpu.VMEM((SEQ, D_MODEL), F32),
            pltpu.VMEM((2, D_MODEL, MIX_RING_COLS), F32),
            pltpu.VMEM((HEADS * CHUNK, SEQ), BF16),
            pltpu.VMEM((SEQ, D_MODEL), BF16),
            pltpu.VMEM((SEQ, D_MODEL), BF16),
            pltpu.VMEM((HEADS_PER_BLOCK * CHUNK, SEQ), F32),
            pltpu.VMEM((SEQ + 2 * CONV_PAD, MIX_COLS), F32),
            pltpu.VMEM((MIX_ROWS, D_MODEL), F32),
            pltpu.SemaphoreType.DMA((2,)),
            pltpu.SemaphoreType.DMA,
            pltpu.SemaphoreType.DMA,
        ],
        compiler_params=pltpu.CompilerParams(
            dimension_semantics=("arbitrary",),
            vmem_limit_bytes=VMEM_LIMIT_BYTES),
        name="mixer",
    )(h, x, w_in, w_proj_a, w_proj_b, w_out, sgu_norm, sgu_w, sgu_b, conv_w)


def kernel(x, ffn1_norm, ffn1_w_in, ffn1_w_out, mix_norm, w_in, sgu_norm, sgu_w, sgu_b, conv_w,
           w_proj_a, w_proj_b, w_out, ffn2_norm, ffn2_w_in, ffn2_w_out, final_norm):
    batch, seq, d = x.shape
    depth = ffn1_norm.shape[0]
    assert (seq, d) == (SEQ, D_MODEL) and w_in.shape[-1] == N_IN and ffn1_w_out.shape[1] == D_FF
    xt = x.reshape(batch * seq, d)
    gains = lambda g: g.reshape(depth, 1, d)
    ffn1_g, mix_g, ffn2_g, sgu_g = gains(ffn1_norm), gains(mix_norm), gains(ffn2_norm), gains(sgu_norm)
    final_g = final_norm.reshape(1, 1, d)
    sgu_b4 = sgu_b.reshape(depth, HEADS, CHUNK, 1)
    conv_w5 = conv_w.reshape(depth, 3, N_MIX_BLOCKS, 1, MIX_COLS)
    for l in range(depth):
        xt, h = _ffn_call(xt, ffn1_g, ffn1_w_in, ffn1_w_out, mix_g, l, l, "emit_h")
        xt = _mixer_call(h, xt, w_in, sgu_g, sgu_w, sgu_b4, conv_w5, w_proj_a, w_proj_b, w_out, l)
        if l == depth - 1:
            (xt,) = _ffn_call(xt, ffn2_g, ffn2_w_in, ffn2_w_out, final_g, l, 0, "final")
        else:
            (xt,) = _ffn_call(xt, ffn2_g, ffn2_w_in, ffn2_w_out, ffn2_g, l, l, "plain")
    return xt.reshape(batch, seq, d)
```

```python
import functools

import numpy as np
import jax
import jax.numpy as jnp
from jax import lax
from jax.experimental import pallas as pl
from jax.experimental.pallas import tpu as pltpu

F32 = jnp.float32
BF16 = jnp.bfloat16
EPS = 1e-6

D_MODEL = 1024
SEQ = 2048
CHUNK = 128
HEADS = 8
HEAD_DIM = 128
D_FF = 2816
N_IN = 7 * D_MODEL

TOKENS_PER_TILE = SEQ
FFN_COLS = 256
FFN_ROWS = 1024
N_FFN_CHUNKS = D_FF // FFN_COLS
FFN_GROUP = 4
N_FFN_GROUPS = -(-N_FFN_CHUNKS // FFN_GROUP)
FFN_SLOTS = 2
MIX_COLS = 256
MIX_ROWS = 1024
CONV_PAD = 8
N_MIX_BLOCKS = D_MODEL // MIX_COLS
N_CHUNKS = SEQ // CHUNK
HEADS_PER_BLOCK = MIX_COLS // HEAD_DIM
VMEM_LIMIT_BYTES = 60 * 1024 * 1024


def _rmsnorm(x, g):
    ms = jnp.mean(x * x, axis=-1, keepdims=True)
    return (x * lax.rsqrt(ms + EPS)) * g


def _gelu_tanh(x):
    c = np.float32(np.sqrt(2.0 / np.pi))
    c1 = np.float32(np.sqrt(2.0 / np.pi) * 0.044715)
    half_x = 0.5 * x
    return half_x + half_x * jnp.tanh(x * (c + c1 * (x * x)))


def _sigmoid(x):
    return 0.5 * jnp.tanh(0.5 * x) + 0.5


def _dot(a, b):
    return jnp.dot(a, b, preferred_element_type=F32)


def _ffn_group_chunks(group):
    return range(group * FFN_GROUP, min((group + 1) * FFN_GROUP, N_FFN_CHUNKS))


def _ffn_group_copies(w_in_hbm, w_out_hbm, wg_buf, wu_buf, wo_buf, sem, layer, slot, group):
    copies = []
    for k, chunk in enumerate(_ffn_group_chunks(group)):
        cols = pl.ds(chunk * FFN_COLS, FFN_COLS)
        up_cols = pl.ds(D_FF + chunk * FFN_COLS, FFN_COLS)
        copies += [
            pltpu.make_async_copy(w_in_hbm.at[layer, :, cols], wg_buf.at[slot, k], sem.at[slot]),
            pltpu.make_async_copy(w_in_hbm.at[layer, :, up_cols], wu_buf.at[slot, k], sem.at[slot]),
            pltpu.make_async_copy(w_out_hbm.at[layer, cols, :], wo_buf.at[slot, k], sem.at[slot]),
        ]
    return copies


def _ffn_kernel(g_ref, g2_ref, x_hbm, w_in_hbm, w_out_hbm, out_hbm, *refs, mode, layer):
    if mode == "emit_h":
        hout_ref, obuf, h_ref, wg_buf, wu_buf, wo_buf, sem, xsem, osem = refs
    else:
        obuf, h_ref, wg_buf, wu_buf, wo_buf, sem, xsem, osem = refs
    i = pl.program_id(0)
    n_tiles = pl.num_programs(0)
    copies = functools.partial(_ffn_group_copies, w_in_hbm, w_out_hbm, wg_buf, wu_buf, wo_buf,
                               sem, layer)
    n_row_chunks = TOKENS_PER_TILE // FFN_ROWS
    tile = obuf.at[i % 2]

    def ring_slot(t, group):
        return (t * N_FFN_GROUPS + group) % FFN_SLOTS

    def start(t, group):
        for cp in copies(ring_slot(t, group), group):
            cp.start()

    def wait(t, group):
        for cp in copies(ring_slot(t, group), group):
            cp.wait()

    def tile_rows(t):
        return pl.ds(pl.multiple_of(t * TOKENS_PER_TILE, TOKENS_PER_TILE), TOKENS_PER_TILE)

    def x_copy(t):
        return pltpu.make_async_copy(x_hbm.at[tile_rows(t)], obuf.at[t % 2], xsem)

    def out_copy(t):
        return pltpu.make_async_copy(obuf.at[t % 2], out_hbm.at[tile_rows(t)], osem)

    @pl.when(i == 0)
    def _():
        x_copy(i).start()
        start(i, 0)

    for g in range(N_FFN_GROUPS):
        wait(i, g)
        if g == 0:
            x_copy(i).wait()
        if g == 1:
            pl.when(i > 0)(lambda: out_copy(i - 1).wait())
            pl.when(i + 1 < n_tiles)(lambda: x_copy(i + 1).start())
        if g + 1 < N_FFN_GROUPS:
            start(i, g + 1)
        else:
            pl.when(i + 1 < n_tiles)(lambda: start(i + 1, 0))
        if g == 0:
            for r in range(n_row_chunks):
                rows = pl.ds(r * FFN_ROWS, FFN_ROWS)
                h_ref[rows, :] = _rmsnorm(tile[rows, :], g_ref[...]).astype(BF16)
        slot = ring_slot(i, g)
        for k, c in enumerate(_ffn_group_chunks(g)):
            wg = wg_buf[slot, k].astype(BF16)
            wu = wu_buf[slot, k].astype(BF16)
            wo = wo_buf[slot, k].astype(BF16)
            for r in range(n_row_chunks):
                rows = pl.ds(r * FFN_ROWS, FFN_ROWS)
                h = h_ref[rows, :]
                gate = _dot(h, wg)
                up = _dot(h, wu)
                act = (0.5 * (gate * _sigmoid(gate)) * up).astype(BF16)
                tile[rows, :] = tile[rows, :] + _dot(act, wo)

    if mode != "plain":
        for r in range(n_row_chunks):
            rows = pl.ds(r * FFN_ROWS, FFN_ROWS)
            y = _rmsnorm(tile[rows, :], g2_ref[...])
            if mode == "emit_h":
                hout_ref[rows, :] = y.astype(BF16)
            else:
                tile[rows, :] = y
    out_copy(i).start()
    pl.when(i == n_tiles - 1)(lambda: out_copy(i).wait())


def _ffn_call(x, norm, w_in, w_out, g2, layer, g2_layer, mode):
    n_tok = x.shape[0]
    nt = n_tok // TOKENS_PER_TILE
    tm = TOKENS_PER_TILE
    hbm = pl.BlockSpec(memory_space=pl.ANY)
    in_specs = [
        pl.BlockSpec((None, 1, D_MODEL), lambda i: (layer, 0, 0)),
        pl.BlockSpec((None, 1, D_MODEL), lambda i: (g2_layer, 0, 0)),
        hbm, hbm, hbm,
    ]
    out_shape = [jax.ShapeDtypeStruct((n_tok, D_MODEL), F32)]
    out_specs = [hbm]
    if mode == "emit_h":
        out_shape.append(jax.ShapeDtypeStruct((n_tok, D_MODEL), BF16))
        out_specs.append(pl.BlockSpec((tm, D_MODEL), lambda i: (i, 0)))
    return pl.pallas_call(
        functools.partial(_ffn_kernel, mode=mode, layer=layer),
        grid=(nt,),
        in_specs=in_specs,
        out_specs=out_specs,
        out_shape=out_shape,
        scratch_shapes=[
            pltpu.VMEM((2, tm, D_MODEL), F32),
            pltpu.VMEM((tm, D_MODEL), BF16),
            pltpu.VMEM((FFN_SLOTS, FFN_GROUP, D_MODEL, FFN_COLS), F32),
            pltpu.VMEM((FFN_SLOTS, FFN_GROUP, D_MODEL, FFN_COLS), F32),
            pltpu.VMEM((FFN_SLOTS, FFN_GROUP, FFN_COLS, D_MODEL), F32),
            pltpu.SemaphoreType.DMA((FFN_SLOTS,)),
            pltpu.SemaphoreType.DMA,
            pltpu.SemaphoreType.DMA,
        ],
        compiler_params=pltpu.CompilerParams(
            dimension_semantics=("arbitrary",),
            vmem_limit_bytes=VMEM_LIMIT_BYTES),
        name=f"ffn_{mode}",
    )(norm, g2, x, w_in, w_out)


MIX_HALF = D_MODEL // 2
MIX_RING_COLS = 2 * D_MODEL
N_MIX_GROUPS = 6
_U, _V, _B1, _B2, _B3, _GA, _GB = (k * D_MODEL for k in range(7))


def _mixer_group_copies(w_in_hbm, wpa_hbm, wpb_hbm, wo_hbm, ring, sem, layer, group):
    slot = group % 2

    def part(p):
        return ring.at[slot, :, pl.ds(p * MIX_HALF, MIX_HALF)]

    def w_in_cols(start):
        return w_in_hbm.at[layer, :, pl.ds(start, MIX_HALF)]

    if group == 0:
        pairs = [(w_in_hbm.at[layer, :, pl.ds(_U, 2 * D_MODEL)], ring.at[slot])]
    elif group in (1, 2):
        off = (group - 1) * MIX_HALF
        pairs = [(w_in_cols(_B1 + off), part(0)), (w_in_cols(_B2 + off), part(1)),
                 (w_in_cols(_B3 + off), part(2))]
    elif group in (3, 4):
        off = (group - 3) * MIX_HALF
        pairs = [(w_in_cols(_GA + off), part(0)), (w_in_cols(_GB + off), part(1)),
                 (wpa_hbm.at[layer, :, pl.ds(off, MIX_HALF)], part(2)),
                 (wpb_hbm.at[layer, :, pl.ds(off, MIX_HALF)], part(3))]
    else:
        pairs = [(wo_hbm.at[layer], ring.at[slot, :, pl.ds(0, D_MODEL)])]
    return [pltpu.make_async_copy(src, dst, sem.at[slot]) for src, dst in pairs]


def _mixer_kernel(h_ref, x_hbm, w_in_hbm, wpa_hbm, wpb_hbm, wo_hbm, sgn_ref, sgw_ref, sgb_ref,
                  cw_ref, out_hbm, obuf, ring, vt_ref, s_ref, tb_ref, g_ref, t_ref, v_ref,
                  wsem, xsem, osem, *, layer):
    i = pl.program_id(0)
    n_tiles = pl.num_programs(0)
    copies = functools.partial(_mixer_group_copies, w_in_hbm, wpa_hbm, wpb_hbm, wo_hbm, ring,
                               wsem, layer)

    def start(group):
        for cp in copies(group):
            cp.start()

    def wait(group):
        for cp in copies(group):
            cp.wait()

    def tile_rows(t):
        return pl.ds(pl.multiple_of(t * SEQ, SEQ), SEQ)

    x_copy = pltpu.make_async_copy(x_hbm.at[tile_rows(i)], obuf, xsem)
    out_copy = pltpu.make_async_copy(obuf, out_hbm.at[tile_rows(i)], osem)

    def weight(slot, col, width=MIX_COLS):
        return ring[slot, :, col:col + width].astype(BF16)

    row_chunks = [slice(r * MIX_ROWS, (r + 1) * MIX_ROWS) for r in range(SEQ // MIX_ROWS)]

    pl.when(i == 0)(lambda: start(0))

    wait(0)
    start(1)
    wv_blocks = [weight(0, _V + n * MIX_COLS) for n in range(N_MIX_BLOCKS)]
    for r, rows in enumerate(row_chunks):
        for n in range(N_MIX_BLOCKS):
            v_ref[:, n * MIX_COLS:(n + 1) * MIX_COLS] = _gelu_tanh(_dot(h_ref[rows, :], wv_blocks[n]))
        vn = _rmsnorm(v_ref[...], sgn_ref[...]).astype(BF16)
        for c in range(MIX_ROWS // CHUNK):
            chunk = r * (MIX_ROWS // CHUNK) + c
            for head in range(HEADS):
                vt_ref[head * CHUNK:(head + 1) * CHUNK, chunk * CHUNK:(chunk + 1) * CHUNK] = (
                    vn[c * CHUNK:(c + 1) * CHUNK, head * HEAD_DIM:(head + 1) * HEAD_DIM])
    zero_block = jnp.zeros((CHUNK, CHUNK), BF16)
    for n in range(N_MIX_BLOCKS):
        h0, h1 = n * HEADS_PER_BLOCK, n * HEADS_PER_BLOCK + 1
        w_pair = jnp.concatenate([
            jnp.concatenate([sgw_ref[h0].astype(BF16), zero_block], axis=1),
            jnp.concatenate([zero_block, sgw_ref[h1].astype(BF16)], axis=1)], axis=0)
        bias = jnp.concatenate([sgb_ref[h0], sgb_ref[h1]], axis=0)
        g_ref[...] = _dot(w_pair, vt_ref[h0 * CHUNK:(h1 + 1) * CHUNK, :]) + bias
        wu = weight(0, n * MIX_COLS)
        for r, rows in enumerate(row_chunks):
            u = _gelu_tanh(_dot(h_ref[rows, :], wu))
            for hh in range(HEADS_PER_BLOCK):
                for c in range(MIX_ROWS // CHUNK):
                    chunk = r * (MIX_ROWS // CHUNK) + c
                    s_ref[chunk * CHUNK:(chunk + 1) * CHUNK,
                          (h0 + hh) * HEAD_DIM:(h0 + hh + 1) * HEAD_DIM] = (
                        u[c * CHUNK:(c + 1) * CHUNK, hh * HEAD_DIM:(hh + 1) * HEAD_DIM]
                        * g_ref[hh * CHUNK:(hh + 1) * CHUNK, chunk * CHUNK:(chunk + 1) * CHUNK]
                    ).astype(BF16)

    t_ref[0:CONV_PAD, :] = jnp.zeros((CONV_PAD, MIX_COLS), F32)
    t_ref[CONV_PAD + SEQ:, :] = jnp.zeros((CONV_PAD, MIX_COLS), F32)
    for group in (1, 2):
        wait(group)
        if group == 1:
            pl.when(i > 0)(lambda: pltpu.make_async_copy(
                obuf, out_hbm.at[tile_rows(i - 1)], osem).wait())
            x_copy.start()
        start(group + 1)
        slot = group % 2
        for q in range(MIX_HALF // MIX_COLS):
            n = (group - 1) * (MIX_HALF // MIX_COLS) + q
            col = q * MIX_COLS
            w_gate, w_c, w_x = (weight(slot, p * MIX_HALF + col) for p in range(3))
            for rows in row_chunks:
                h = h_ref[rows, :]
                t_ref[CONV_PAD + rows.start:CONV_PAD + rows.stop, :] = _dot(h, w_c) * _dot(h, w_x)
            for rows in row_chunks:
                lo = CONV_PAD + rows.start
                conv = (cw_ref[0, n] * t_ref[lo - 1:lo - 1 + MIX_ROWS, :]
                        + cw_ref[1, n] * t_ref[lo:lo + MIX_ROWS, :]
                        + cw_ref[2, n] * t_ref[lo + 1:lo + 1 + MIX_ROWS, :])
                tb_ref[rows, n * MIX_COLS:(n + 1) * MIX_COLS] = (
                    _dot(h_ref[rows, :], w_gate) * conv).astype(BF16)

    for group in (3, 4):
        wait(group)
        start(group + 1)
        slot = group % 2
        for q in range(MIX_HALF // MIX_COLS):
            n = (group - 3) * (MIX_HALF // MIX_COLS) + q
            col = q * MIX_COLS
            w_ga, w_gb, w_pa, w_pb = (weight(slot, p * MIX_HALF + col) for p in range(4))
            for r, rows in enumerate(row_chunks):
                h = h_ref[rows, :]
                merged = _sigmoid(_dot(h, w_ga)) * _dot(s_ref[rows, :], w_pa)
                merged = merged + _sigmoid(_dot(h, w_gb)) * _dot(tb_ref[rows, :], w_pb)
                half, local = divmod(rows.start, SEQ // 2)
                vt_ref[local:local + MIX_ROWS,
                       half * D_MODEL + n * MIX_COLS:half * D_MODEL + (n + 1) * MIX_COLS] = (
                    merged.astype(BF16))

    wait(5)
    x_copy.wait()
    pl.when(i + 1 < n_tiles)(lambda: start(0))
    for n in range(N_MIX_BLOCKS):
        cols = slice(n * MIX_COLS, (n + 1) * MIX_COLS)
        w = weight(1, n * MIX_COLS)
        for rows in row_chunks:
            half, local = divmod(rows.start, SEQ // 2)
            lhs = vt_ref[local:local + MIX_ROWS, half * D_MODEL:(half + 1) * D_MODEL]
            obuf[rows, cols] = obuf[rows, cols] + _dot(lhs, w)
    out_copy.start()
    pl.when(i == n_tiles - 1)(lambda: out_copy.wait())


def _mixer_call(h, x, w_in, sgu_norm, sgu_w, sgu_b, conv_w, w_proj_a, w_proj_b, w_out, layer):
    n_tok = x.shape[0]
    nt = n_tok // SEQ
    nb = N_MIX_BLOCKS
    hbm = pl.BlockSpec(memory_space=pl.ANY)
    in_specs = [
        pl.BlockSpec((SEQ, D_MODEL), lambda i: (i, 0)),
        hbm, hbm, hbm, hbm, hbm,
        pl.BlockSpec((None, 1, D_MODEL), lambda i: (layer, 0, 0)),
        pl.BlockSpec((None, HEADS, CHUNK, CHUNK), lambda i: (layer, 0, 0, 0)),
        pl.BlockSpec((None, HEADS, CHUNK, 1), lambda i: (layer, 0, 0, 0)),
        pl.BlockSpec((None, 3, nb, 1, MIX_COLS), lambda i: (layer, 0, 0, 0, 0)),
    ]
    return pl.pallas_call(
        functools.partial(_mixer_kernel, layer=layer),
        grid=(nt,),
        in_specs=in_specs,
        out_specs=hbm,
        out_shape=jax.ShapeDtypeStruct((n_tok, D_MODEL), F32),
        scratch_shapes=[
            pltpu.VMEM((SEQ, D_MODEL), F32),
            pltpu.VMEM((2, D_MODEL, MIX_RING_COLS), F32),
            pltpu.VMEM((HEADS * CHUNK, SEQ), BF16),
            pltpu.VMEM((SEQ, D_MODEL), BF16),
            pltpu.VMEM((SEQ, D_MODEL), BF16),
            pltpu.VMEM((HEADS_PER_BLOCK * CHUNK, SEQ), F32),
            pltpu.VMEM((SEQ + 2 * CONV_PAD, MIX_COLS), F32),
            pltpu.VMEM((MIX_ROWS, D_MODEL), F32),
            pltpu.SemaphoreType.DMA((2,)),
            pltpu.SemaphoreType.DMA,
            pltpu.SemaphoreType.DMA,
        ],
        compiler_params=pltpu.CompilerParams(
            dimension_semantics=("arbitrary",),
            vmem_limit_bytes=VMEM_LIMIT_BYTES),
        name="mixer",
    )(h, x, w_in, w_proj_a, w_proj_b, w_out, sgu_norm, sgu_w, sgu_b, conv_w)


def kernel(x, ffn1_norm, ffn1_w_in, ffn1_w_out, mix_norm, w_in, sgu_norm, sgu_w, sgu_b, conv_w,
           w_proj_a, w_proj_b, w_out, ffn2_norm, ffn2_w_in, ffn2_w_out, final_norm):
    batch, seq, d = x.shape
    depth = ffn1_norm.shape[0]
    assert (seq, d) == (SEQ, D_MODEL) and w_in.shape[-1] == N_IN and ffn1_w_out.shape[1] == D_FF
    xt = x.reshape(batch * seq, d)
    gains = lambda g: g.reshape(depth, 1, d)
    ffn1_g, mix_g, ffn2_g, sgu_g = gains(ffn1_norm), gains(mix_norm), gains(ffn2_norm), gains(sgu_norm)
    final_g = final_norm.reshape(1, 1, d)
    sgu_b4 = sgu_b.reshape(depth, HEADS, CHUNK, 1)
    conv_w5 = conv_w.reshape(depth, 3, N_MIX_BLOCKS, 1, MIX_COLS)
    for l in range(depth):
        xt, h = _ffn_call(xt, ffn1_g, ffn1_w_in, ffn1_w_out, mix_g, l, l, "emit_h")
        xt = _mixer_call(h, xt, w_in, sgu_g, sgu_w, sgu_b4, conv_w5, w_proj_a, w_proj_b, w_out, l)
        if l == depth - 1:
            (xt,) = _ffn_call(xt, ffn2_g, ffn2_w_in, ffn2_w_out, final_g, l, 0, "final")
        else:
            (xt,) = _ffn_call(xt, ffn2_g, ffn2_w_in, ffn2_w_out, ffn2_g, l, l, "plain")
    return xt.reshape(batch, seq, d)
```

```python
import functools

import numpy as np
import jax
import jax.numpy as jnp
from jax import lax
from jax.experimental import pallas as pl
from jax.experimental.pallas import tpu as pltpu

F32 = jnp.float32
BF16 = jnp.bfloat16
EPS = 1e-6

D_MODEL = 1024
SEQ = 2048
CHUNK = 128
HEADS = 8
HEAD_DIM = 128
D_FF = 2816
N_IN = 7 * D_MODEL

TOKENS_PER_TILE = SEQ
FFN_COLS = 256
FFN_ROWS = 1024
N_FFN_CHUNKS = D_FF // FFN_COLS
FFN_GROUP = 4
N_FFN_GROUPS = -(-N_FFN_CHUNKS // FFN_GROUP)
FFN_SLOTS = 2
MIX_COLS = 256
MIX_ROWS = 1024
CONV_PAD = 8
N_MIX_BLOCKS = D_MODEL // MIX_COLS
N_CHUNKS = SEQ // CHUNK
HEADS_PER_BLOCK = MIX_COLS // HEAD_DIM
VMEM_LIMIT_BYTES = 60 * 1024 * 1024


def _rmsnorm(x, g):
    ms = jnp.mean(x * x, axis=-1, keepdims=True)
    return (x * lax.rsqrt(ms + EPS)) * g


def _gelu_tanh(x):
    c = np.float32(np.sqrt(2.0 / np.pi))
    c1 = np.float32(np.sqrt(2.0 / np.pi) * 0.044715)
    half_x = 0.5 * x
    return half_x + half_x * jnp.tanh(x * (c + c1 * (x * x)))


def _sigmoid(x):
    return 0.5 * jnp.tanh(0.5 * x) + 0.5


def _dot(a, b):
    return jnp.dot(a, b, preferred_element_type=F32)


def _ffn_group_chunks(group):
    return range(group * FFN_GROUP, min((group + 1) * FFN_GROUP, N_FFN_CHUNKS))


def _ffn_group_copies(w_in_hbm, w_out_hbm, wg_buf, wu_buf, wo_buf, sem, layer, slot, group):
    copies = []
    for k, chunk in enumerate(_ffn_group_chunks(group)):
        cols = pl.ds(chunk * FFN_COLS, FFN_COLS)
        up_cols = pl.ds(D_FF + chunk * FFN_COLS, FFN_COLS)
        copies += [
            pltpu.make_async_copy(w_in_hbm.at[layer, :, cols], wg_buf.at[slot, k], sem.at[slot]),
            pltpu.make_async_copy(w_in_hbm.at[layer, :, up_cols], wu_buf.at[slot, k], sem.at[slot]),
            pltpu.make_async_copy(w_out_hbm.at[layer, cols, :], wo_buf.at[slot, k], sem.at[slot]),
        ]
    return copies


def _ffn_kernel(g_ref, g2_ref, x_hbm, w_in_hbm, w_out_hbm, out_hbm, *refs, mode, layer):
    if mode == "emit_h":
        hout_ref, obuf, h_ref, act_ref, wg_buf, wu_buf, wo_buf, sem, xsem, osem = refs
    else:
        obuf, h_ref, act_ref, wg_buf, wu_buf, wo_buf, sem, xsem, osem = refs
    i = pl.program_id(0)
    n_tiles = pl.num_programs(0)
    copies = functools.partial(_ffn_group_copies, w_in_hbm, w_out_hbm, wg_buf, wu_buf, wo_buf,
                               sem, layer)
    n_row_chunks = TOKENS_PER_TILE // FFN_ROWS
    tile = obuf.at[i % 2]

    def ring_slot(t, group):
        return (t * N_FFN_GROUPS + group) % FFN_SLOTS

    def start(t, group):
        for cp in copies(ring_slot(t, group), group):
            cp.start()

    def wait(t, group):
        for cp in copies(ring_slot(t, group), group):
            cp.wait()

    def tile_rows(t):
        return pl.ds(pl.multiple_of(t * TOKENS_PER_TILE, TOKENS_PER_TILE), TOKENS_PER_TILE)

    def x_copy(t):
        return pltpu.make_async_copy(x_hbm.at[tile_rows(t)], obuf.at[t % 2], xsem)

    def out_copy(t):
        return pltpu.make_async_copy(obuf.at[t % 2], out_hbm.at[tile_rows(t)], osem)

    @pl.when(i == 0)
    def _():
        x_copy(i).start()
        start(i, 0)

    for g in range(N_FFN_GROUPS):
        wait(i, g)
        if g == 0:
            x_copy(i).wait()
        if g == 1:
            pl.when(i > 0)(lambda: out_copy(i - 1).wait())
            pl.when(i + 1 < n_tiles)(lambda: x_copy(i + 1).start())
        if g + 1 < N_FFN_GROUPS:
            start(i, g + 1)
        else:
            pl.when(i + 1 < n_tiles)(lambda: start(i + 1, 0))
        if g == 0:
            for r in range(n_row_chunks):
                rows = pl.ds(r * FFN_ROWS, FFN_ROWS)
                h_ref[rows, :] = _rmsnorm(tile[rows, :], g_ref[...]).astype(BF16)
        slot = ring_slot(i, g)
        n_chunks = len(_ffn_group_chunks(g))
        for k in range(n_chunks):
            wg = wg_buf[slot, k].astype(BF16)
            wu = wu_buf[slot, k].astype(BF16)
            for r in range(n_row_chunks):
                rows = pl.ds(r * FFN_ROWS, FFN_ROWS)
                h = h_ref[rows, :]
                gate = _dot(h, wg)
                up = _dot(h, wu)
                act_ref[rows, k * FFN_COLS:(k + 1) * FFN_COLS] = (
                    0.5 * (gate * _sigmoid(gate)) * up).astype(BF16)
        wo = wo_buf[slot, 0:n_chunks].reshape(n_chunks * FFN_COLS, D_MODEL).astype(BF16)
        for r in range(n_row_chunks):
            rows = pl.ds(r * FFN_ROWS, FFN_ROWS)
            tile[rows, :] = tile[rows, :] + _dot(act_ref[rows, 0:n_chunks * FFN_COLS], wo)

    if mode != "plain":
        for r in range(n_row_chunks):
            rows = pl.ds(r * FFN_ROWS, FFN_ROWS)
            y = _rmsnorm(tile[rows, :], g2_ref[...])
            if mode == "emit_h":
                hout_ref[rows, :] = y.astype(BF16)
            else:
                tile[rows, :] = y
    out_copy(i).start()
    pl.when(i == n_tiles - 1)(lambda: out_copy(i).wait())


def _ffn_call(x, norm, w_in, w_out, g2, layer, g2_layer, mode):
    n_tok = x.shape[0]
    nt = n_tok // TOKENS_PER_TILE
    tm = TOKENS_PER_TILE
    hbm = pl.BlockSpec(memory_space=pl.ANY)
    in_specs = [
        pl.BlockSpec((None, 1, D_MODEL), lambda i: (layer, 0, 0)),
        pl.BlockSpec((None, 1, D_MODEL), lambda i: (g2_layer, 0, 0)),
        hbm, hbm, hbm,
    ]
    out_shape = [jax.ShapeDtypeStruct((n_tok, D_MODEL), F32)]
    out_specs = [hbm]
    if mode == "emit_h":
        out_shape.append(jax.ShapeDtypeStruct((n_tok, D_MODEL), BF16))
        out_specs.append(pl.BlockSpec((tm, D_MODEL), lambda i: (i, 0)))
    return pl.pallas_call(
        functools.partial(_ffn_kernel, mode=mode, layer=layer),
        grid=(nt,),
        in_specs=in_specs,
        out_specs=out_specs,
        out_shape=out_shape,
        scratch_shapes=[
            pltpu.VMEM((2, tm, D_MODEL), F32),
            pltpu.VMEM((tm, D_MODEL), BF16),
            pltpu.VMEM((tm, FFN_GROUP * FFN_COLS), BF16),
            pltpu.VMEM((FFN_SLOTS, FFN_GROUP, D_MODEL, FFN_COLS), F32),
            pltpu.VMEM((FFN_SLOTS, FFN_GROUP, D_MODEL, FFN_COLS), F32),
            pltpu.VMEM((FFN_SLOTS, FFN_GROUP, FFN_COLS, D_MODEL), F32),
            pltpu.SemaphoreType.DMA((FFN_SLOTS,)),
            pltpu.SemaphoreType.DMA,
            pltpu.SemaphoreType.DMA,
        ],
        compiler_params=pltpu.CompilerParams(
            dimension_semantics=("arbitrary",),
            vmem_limit_bytes=VMEM_LIMIT_BYTES),
        name=f"ffn_{mode}",
    )(norm, g2, x, w_in, w_out)


MIX_HALF = D_MODEL // 2
MIX_RING_COLS = 2 * D_MODEL
N_MIX_GROUPS = 6
_U, _V, _B1, _B2, _B3, _GA, _GB = (k * D_MODEL for k in range(7))


def _mixer_group_copies(w_in_hbm, wpa_hbm, wpb_hbm, wo_hbm, ring, sem, layer, group):
    slot = group % 2

    def part(p):
        return ring.at[slot, :, pl.ds(p * MIX_HALF, MIX_HALF)]

    def w_in_cols(start):
        return w_in_hbm.at[layer, :, pl.ds(start, MIX_HALF)]

    if group == 0:
        pairs = [(w_in_hbm.at[layer, :, pl.ds(_U, 2 * D_MODEL)], ring.at[slot])]
    elif group in (1, 2):
        off = (group - 1) * MIX_HALF
        pairs = [(w_in_cols(_B1 + off), part(0)), (w_in_cols(_B2 + off), part(1)),
                 (w_in_cols(_B3 + off), part(2))]
    elif group in (3, 4):
        off = (group - 3) * MIX_HALF
        pairs = [(w_in_cols(_GA + off), part(0)), (w_in_cols(_GB + off), part(1)),
                 (wpa_hbm.at[layer, :, pl.ds(off, MIX_HALF)], part(2)),
                 (wpb_hbm.at[layer, :, pl.ds(off, MIX_HALF)], part(3))]
    else:
        pairs = [(wo_hbm.at[layer], ring.at[slot, :, pl.ds(0, D_MODEL)])]
    return [pltpu.make_async_copy(src, dst, sem.at[slot]) for src, dst in pairs]


def _mixer_kernel(h_ref, x_hbm, w_in_hbm, wpa_hbm, wpb_hbm, wo_hbm, sgn_ref, sgw_ref, sgb_ref,
                  cw_ref, out_hbm, obuf, ring, vt_ref, s_ref, tb_ref, g_ref, t_ref, v_ref,
                  wsem, xsem, osem, *, layer):
    i = pl.program_id(0)
    n_tiles = pl.num_programs(0)
    copies = functools.partial(_mixer_group_copies, w_in_hbm, wpa_hbm, wpb_hbm, wo_hbm, ring,
                               wsem, layer)

    def start(group):
        for cp in copies(group):
            cp.start()

    def wait(group):
        for cp in copies(group):
            cp.wait()

    def tile_rows(t):
        return pl.ds(pl.multiple_of(t * SEQ, SEQ), SEQ)

    x_copy = pltpu.make_async_copy(x_hbm.at[tile_rows(i)], obuf, xsem)
    out_copy = pltpu.make_async_copy(obuf, out_hbm.at[tile_rows(i)], osem)

    def weight(slot, col, width=MIX_COLS):
        return ring[slot, :, col:col + width].astype(BF16)

    row_chunks = [slice(r * MIX_ROWS, (r + 1) * MIX_ROWS) for r in range(SEQ // MIX_ROWS)]

    pl.when(i == 0)(lambda: start(0))

    wait(0)
    start(1)
    wv_blocks = [weight(0, _V + n * MIX_COLS) for n in range(N_MIX_BLOCKS)]
    for r, rows in enumerate(row_chunks):
        for n in range(N_MIX_BLOCKS):
            v_ref[:, n * MIX_COLS:(n + 1) * MIX_COLS] = _gelu_tanh(_dot(h_ref[rows, :], wv_blocks[n]))
        vn = _rmsnorm(v_ref[...], sgn_ref[...]).astype(BF16)
        for c in range(MIX_ROWS // CHUNK):
            chunk = r * (MIX_ROWS // CHUNK) + c
            for head in range(HEADS):
                vt_ref[head * CHUNK:(head + 1) * CHUNK, chunk * CHUNK:(chunk + 1) * CHUNK] = (
                    vn[c * CHUNK:(c + 1) * CHUNK, head * HEAD_DIM:(head + 1) * HEAD_DIM])
    zero_block = jnp.zeros((CHUNK, CHUNK), BF16)
    for n in range(N_MIX_BLOCKS):
        h0, h1 = n * HEADS_PER_BLOCK, n * HEADS_PER_BLOCK + 1
        w_pair = jnp.concatenate([
            jnp.concatenate([sgw_ref[h0].astype(BF16), zero_block], axis=1),
            jnp.concatenate([zero_block, sgw_ref[h1].astype(BF16)], axis=1)], axis=0)
        bias = jnp.concatenate([sgb_ref[h0], sgb_ref[h1]], axis=0)
        g_ref[...] = _dot(w_pair, vt_ref[h0 * CHUNK:(h1 + 1) * CHUNK, :]) + bias
        wu = weight(0, n * MIX_COLS)
        for r, rows in enumerate(row_chunks):
            u = _gelu_tanh(_dot(h_ref[rows, :], wu))
            for hh in range(HEADS_PER_BLOCK):
                for c in range(MIX_ROWS // CHUNK):
                    chunk = r * (MIX_ROWS // CHUNK) + c
                    s_ref[chunk * CHUNK:(chunk + 1) * CHUNK,
                          (h0 + hh) * HEAD_DIM:(h0 + hh + 1) * HEAD_DIM] = (
                        u[c * CHUNK:(c + 1) * CHUNK, hh * HEAD_DIM:(hh + 1) * HEAD_DIM]
                        * g_ref[hh * CHUNK:(hh + 1) * CHUNK, chunk * CHUNK:(chunk + 1) * CHUNK]
                    ).astype(BF16)

    t_ref[0:CONV_PAD, :] = jnp.zeros((CONV_PAD, MIX_COLS), F32)
    t_ref[CONV_PAD + SEQ:, :] = jnp.zeros((CONV_PAD, MIX_COLS), F32)
    for group in (1, 2):
        wait(group)
        if group == 1:
            pl.when(i > 0)(lambda: pltpu.make_async_copy(
                obuf, out_hbm.at[tile_rows(i - 1)], osem).wait())
            x_copy.start()
        start(group + 1)
        slot = group % 2
        for q in range(MIX_HALF // MIX_COLS):
            n = (group - 1) * (MIX_HALF // MIX_COLS) + q
            col = q * MIX_COLS
            w_gate, w_c, w_x = (weight(slot, p * MIX_HALF + col) for p in range(3))
            for rows in row_chunks:
                h = h_ref[rows, :]
                t_ref[CONV_PAD + rows.start:CONV_PAD + rows.stop, :] = _dot(h, w_c) * _dot(h, w_x)
            for rows in row_chunks:
                lo = CONV_PAD + rows.start
                conv = (cw_ref[0, n] * t_ref[lo - 1:lo - 1 + MIX_ROWS, :]
                        + cw_ref[1, n] * t_ref[lo:lo + MIX_ROWS, :]
                        + cw_ref[2, n] * t_ref[lo + 1:lo + 1 + MIX_ROWS, :])
                tb_ref[rows, n * MIX_COLS:(n + 1) * MIX_COLS] = (
                    _dot(h_ref[rows, :], w_gate) * conv).astype(BF16)

    for group in (3, 4):
        wait(group)
        start(group + 1)
        slot = group % 2
        for q in range(MIX_HALF // MIX_COLS):
            n = (group - 3) * (MIX_HALF // MIX_COLS) + q
            col = q * MIX_COLS
            w_ga, w_gb, w_pa, w_pb = (weight(slot, p * MIX_HALF + col) for p in range(4))
            for r, rows in enumerate(row_chunks):
                h = h_ref[rows, :]
                merged = _sigmoid(_dot(h, w_ga)) * _dot(s_ref[rows, :], w_pa)
                merged = merged + _sigmoid(_dot(h, w_gb)) * _dot(tb_ref[rows, :], w_pb)
                half, local = divmod(rows.start, SEQ // 2)
                vt_ref[local:local + MIX_ROWS,
                       half * D_MODEL + n * MIX_COLS:half * D_MODEL + (n + 1) * MIX_COLS] = (
                    merged.astype(BF16))

    wait(5)
    x_copy.wait()
    pl.when(i + 1 < n_tiles)(lambda: start(0))
    for n in range(N_MIX_BLOCKS):
        cols = slice(n * MIX_COLS, (n + 1) * MIX_COLS)
        w = weight(1, n * MIX_COLS)
        for rows in row_chunks:
            half, local = divmod(rows.start, SEQ // 2)
            lhs = vt_ref[local:local + MIX_ROWS, half * D_MODEL:(half + 1) * D_MODEL]
            obuf[rows, cols] = obuf[rows, cols] + _dot(lhs, w)
    out_copy.start()
    pl.when(i == n_tiles - 1)(lambda: out_copy.wait())


def _mixer_call(h, x, w_in, sgu_norm, sgu_w, sgu_b, conv_w, w_proj_a, w_proj_b, w_out, layer):
    n_tok = x.shape[0]
    nt = n_tok // SEQ
    nb = N_MIX_BLOCKS
    hbm = pl.BlockSpec(memory_space=pl.ANY)
    in_specs = [
        pl.BlockSpec((SEQ, D_MODEL), lambda i: (i, 0)),
        hbm, hbm, hbm, hbm, hbm,
        pl.BlockSpec((None, 1, D_MODEL), lambda i: (layer, 0, 0)),
        pl.BlockSpec((None, HEADS, CHUNK, CHUNK), lambda i: (layer, 0, 0, 0)),
        pl.BlockSpec((None, HEADS, CHUNK, 1), lambda i: (layer, 0, 0, 0)),
        pl.BlockSpec((None, 3, nb, 1, MIX_COLS), lambda i: (layer, 0, 0, 0, 0)),
    ]
    return pl.pallas_call(
        functools.partial(_mixer_kernel, layer=layer),
        grid=(nt,),
        in_specs=in_specs,
        out_specs=hbm,
        out_shape=jax.ShapeDtypeStruct((n_tok, D_MODEL), F32),
        scratch_shapes=[
            pltpu.VMEM((SEQ, D_MODEL), F32),
            pltpu.VMEM((2, D_MODEL, MIX_RING_COLS), F32),
            pltpu.VMEM((HEADS * CHUNK, SEQ), BF16),
            pltpu.VMEM((SEQ, D_MODEL), BF16),
            pltpu.VMEM((SEQ, D_MODEL), BF16),
            pltpu.VMEM((HEADS_PER_BLOCK * CHUNK, SEQ), F32),
            pltpu.VMEM((SEQ + 2 * CONV_PAD, MIX_COLS), F32),
            pltpu.VMEM((MIX_ROWS, D_MODEL), F32),
            pltpu.SemaphoreType.DMA((2,)),
            pltpu.SemaphoreType.DMA,
            pltpu.SemaphoreType.DMA,
        ],
        compiler_params=pltpu.CompilerParams(
            dimension_semantics=("arbitrary",),
            vmem_limit_bytes=VMEM_LIMIT_BYTES),
        name="mixer",
    )(h, x, w_in, w_proj_a, w_proj_b, w_out, sgu_norm, sgu_w, sgu_b, conv_w)


def kernel(x, ffn1_norm, ffn1_w_in, ffn1_w_out, mix_norm, w_in, sgu_norm, sgu_w, sgu_b, conv_w,
           w_proj_a, w_proj_b, w_out, ffn2_norm, ffn2_w_in, ffn2_w_out, final_norm):
    batch, seq, d = x.shape
    depth = ffn1_norm.shape[0]
    assert (seq, d) == (SEQ, D_MODEL) and w_in.shape[-1] == N_IN and ffn1_w_out.shape[1] == D_FF
    xt = x.reshape(batch * seq, d)
    gains = lambda g: g.reshape(depth, 1, d)
    ffn1_g, mix_g, ffn2_g, sgu_g = gains(ffn1_norm), gains(mix_norm), gains(ffn2_norm), gains(sgu_norm)
    final_g = final_norm.reshape(1, 1, d)
    sgu_b4 = sgu_b.reshape(depth, HEADS, CHUNK, 1)
    conv_w5 = conv_w.reshape(depth, 3, N_MIX_BLOCKS, 1, MIX_COLS)
    for l in range(depth):
        xt, h = _ffn_call(xt, ffn1_g, ffn1_w_in, ffn1_w_out, mix_g, l, l, "emit_h")
        xt = _mixer_call(h, xt, w_in, sgu_g, sgu_w, sgu_b4, conv_w5, w_proj_a, w_proj_b, w_out, l)
        if l == depth - 1:
            (xt,) = _ffn_call(xt, ffn2_g, ffn2_w_in, ffn2_w_out, final_g, l, 0, "final")
        else:
            (xt,) = _ffn_call(xt, ffn2_g, ffn2_w_in, ffn2_w_out, ffn2_g, l, l, "plain")
    return xt.reshape(batch, seq, d)
```

```python
import functools

import numpy as np
import jax
import jax.numpy as jnp
from jax import lax
from jax.experimental import pallas as pl
from jax.experimental.pallas import tpu as pltpu

F32 = jnp.float32
BF16 = jnp.bfloat16
EPS = 1e-6

D_MODEL = 1024
SEQ = 2048
CHUNK = 128
HEADS = 8
HEAD_DIM = 128
D_FF = 2816
N_IN = 7 * D_MODEL

TOKENS_PER_TILE = SEQ
FFN_COLS = 256
FFN_ROWS = 1024
N_FFN_CHUNKS = D_FF // FFN_COLS
FFN_GROUP = 4
N_FFN_GROUPS = -(-N_FFN_CHUNKS // FFN_GROUP)
FFN_SLOTS = 2
MIX_COLS = 256
MIX_ROWS = 1024
CONV_PAD = 8
N_MIX_BLOCKS = D_MODEL // MIX_COLS
N_CHUNKS = SEQ // CHUNK
HEADS_PER_BLOCK = MIX_COLS // HEAD_DIM
VMEM_LIMIT_BYTES = 60 * 1024 * 1024


def _rmsnorm(x, g):
    ms = jnp.mean(x * x, axis=-1, keepdims=True)
    return (x * lax.rsqrt(ms + EPS)) * g


def _gelu_tanh(x):
    c = np.float32(np.sqrt(2.0 / np.pi))
    c1 = np.float32(np.sqrt(2.0 / np.pi) * 0.044715)
    half_x = 0.5 * x
    return half_x + half_x * jnp.tanh(x * (c + c1 * (x * x)))


def _sigmoid(x):
    return 0.5 * jnp.tanh(0.5 * x) + 0.5


def _dot(a, b):
    return jnp.dot(a, b, preferred_element_type=F32)


def _ffn_group_chunks(group):
    return range(group * FFN_GROUP, min((group + 1) * FFN_GROUP, N_FFN_CHUNKS))


def _ffn_group_copies(w_in_hbm, w_out_hbm, wg_buf, wu_buf, wo_buf, sem, layer, slot, group):
    copies = []
    for k, chunk in enumerate(_ffn_group_chunks(group)):
        cols = pl.ds(chunk * FFN_COLS, FFN_COLS)
        up_cols = pl.ds(D_FF + chunk * FFN_COLS, FFN_COLS)
        copies += [
            pltpu.make_async_copy(w_in_hbm.at[layer, :, cols], wg_buf.at[slot, k], sem.at[slot]),
            pltpu.make_async_copy(w_in_hbm.at[layer, :, up_cols], wu_buf.at[slot, k], sem.at[slot]),
            pltpu.make_async_copy(w_out_hbm.at[layer, cols, :], wo_buf.at[slot, k], sem.at[slot]),
        ]
    return copies


def _ffn_kernel(g_ref, g2_ref, x_hbm, w_in_hbm, w_out_hbm, out_hbm, *refs, mode, layer):
    if mode == "emit_h":
        hout_ref, obuf, h_ref, act_ref, wg_buf, wu_buf, wo_buf, sem, xsem, osem = refs
    else:
        obuf, h_ref, act_ref, wg_buf, wu_buf, wo_buf, sem, xsem, osem = refs
    i = pl.program_id(0)
    n_tiles = pl.num_programs(0)
    copies = functools.partial(_ffn_group_copies, w_in_hbm, w_out_hbm, wg_buf, wu_buf, wo_buf,
                               sem, layer)
    n_row_chunks = TOKENS_PER_TILE // FFN_ROWS
    tile = obuf.at[i % 2]

    def ring_slot(t, group):
        return (t * N_FFN_GROUPS + group) % FFN_SLOTS

    def start(t, group):
        for cp in copies(ring_slot(t, group), group):
            cp.start()

    def wait(t, group):
        for cp in copies(ring_slot(t, group), group):
            cp.wait()

    def tile_rows(t):
        return pl.ds(pl.multiple_of(t * TOKENS_PER_TILE, TOKENS_PER_TILE), TOKENS_PER_TILE)

    def x_copy(t):
        return pltpu.make_async_copy(x_hbm.at[tile_rows(t)], obuf.at[t % 2], xsem)

    def out_copy(t):
        return pltpu.make_async_copy(obuf.at[t % 2], out_hbm.at[tile_rows(t)], osem)

    @pl.when(i == 0)
    def _():
        x_copy(i).start()
        start(i, 0)

    for g in range(N_FFN_GROUPS):
        wait(i, g)
        if g == 0:
            x_copy(i).wait()
        if g == 1:
            pl.when(i > 0)(lambda: out_copy(i - 1).wait())
            pl.when(i + 1 < n_tiles)(lambda: x_copy(i + 1).start())
        if g + 1 < N_FFN_GROUPS:
            start(i, g + 1)
        else:
            pl.when(i + 1 < n_tiles)(lambda: start(i + 1, 0))
        if g == 0:
            for r in range(n_row_chunks):
                rows = pl.ds(r * FFN_ROWS, FFN_ROWS)
                h_ref[rows, :] = _rmsnorm(tile[rows, :], g_ref[...]).astype(BF16)
        slot = ring_slot(i, g)
        n_chunks = len(_ffn_group_chunks(g))
        for k in range(n_chunks):
            wg = wg_buf[slot, k].astype(BF16)
            wu = wu_buf[slot, k].astype(BF16)
            for r in range(n_row_chunks):
                rows = pl.ds(r * FFN_ROWS, FFN_ROWS)
                h = h_ref[rows, :]
                gate = _dot(h, wg)
                up = _dot(h, wu)
                act_ref[rows, k * FFN_COLS:(k + 1) * FFN_COLS] = (
                    0.5 * (gate * _sigmoid(gate)) * up).astype(BF16)
        wo = wo_buf[slot, 0:n_chunks].reshape(n_chunks * FFN_COLS, D_MODEL).astype(BF16)
        for r in range(n_row_chunks):
            rows = pl.ds(r * FFN_ROWS, FFN_ROWS)
            tile[rows, :] = tile[rows, :] + _dot(act_ref[rows, 0:n_chunks * FFN_COLS], wo)

    if mode != "plain":
        for r in range(n_row_chunks):
            rows = pl.ds(r * FFN_ROWS, FFN_ROWS)
            y = _rmsnorm(tile[rows, :], g2_ref[...])
            if mode == "emit_h":
                hout_ref[rows, :] = y.astype(BF16)
            else:
                tile[rows, :] = y
    out_copy(i).start()
    pl.when(i == n_tiles - 1)(lambda: out_copy(i).wait())


def _ffn_call(x, norm, w_in, w_out, g2, layer, g2_layer, mode):
    n_tok = x.shape[0]
    nt = n_tok // TOKENS_PER_TILE
    tm = TOKENS_PER_TILE
    hbm = pl.BlockSpec(memory_space=pl.ANY)
    in_specs = [
        pl.BlockSpec((None, 1, D_MODEL), lambda i: (layer, 0, 0)),
        pl.BlockSpec((None, 1, D_MODEL), lambda i: (g2_layer, 0, 0)),
        hbm, hbm, hbm,
    ]
    out_shape = [jax.ShapeDtypeStruct((n_tok, D_MODEL), F32)]
    out_specs = [hbm]
    if mode == "emit_h":
        out_shape.append(jax.ShapeDtypeStruct((n_tok, D_MODEL), BF16))
        out_specs.append(pl.BlockSpec((tm, D_MODEL), lambda i: (i, 0)))
    return pl.pallas_call(
        functools.partial(_ffn_kernel, mode=mode, layer=layer),
        grid=(nt,),
        in_specs=in_specs,
        out_specs=out_specs,
        out_shape=out_shape,
        scratch_shapes=[
            pltpu.VMEM((2, tm, D_MODEL), F32),
            pltpu.VMEM((tm, D_MODEL), BF16),
            pltpu.VMEM((tm, FFN_GROUP * FFN_COLS), BF16),
            pltpu.VMEM((FFN_SLOTS, FFN_GROUP, D_MODEL, FFN_COLS), F32),
            pltpu.VMEM((FFN_SLOTS, FFN_GROUP, D_MODEL, FFN_COLS), F32),
            pltpu.VMEM((FFN_SLOTS, FFN_GROUP, FFN_COLS, D_MODEL), F32),
            pltpu.SemaphoreType.DMA((FFN_SLOTS,)),
            pltpu.SemaphoreType.DMA,
            pltpu.SemaphoreType.DMA,
        ],
        compiler_params=pltpu.CompilerParams(
            dimension_semantics=("arbitrary",),
            vmem_limit_bytes=VMEM_LIMIT_BYTES),
        name=f"ffn_{mode}",
    )(norm, g2, x, w_in, w_out)


MIX_HALF = D_MODEL // 2
MIX_RING_COLS = 2 * D_MODEL
N_MIX_GROUPS = 6
_U, _V, _B1, _B2, _B3, _GA, _GB = (k * D_MODEL for k in range(7))


def _mixer_group_copies(w_in_hbm, wpa_hbm, wpb_hbm, wo_hbm, ring, sem, layer, group):
    slot = group % 2

    def part(p):
        return ring.at[slot, :, pl.ds(p * MIX_HALF, MIX_HALF)]

    def w_in_cols(start):
        return w_in_hbm.at[layer, :, pl.ds(start, MIX_HALF)]

    if group == 0:
        pairs = [(w_in_hbm.at[layer, :, pl.ds(_U, 2 * D_MODEL)], ring.at[slot])]
    elif group in (1, 2):
        off = (group - 1) * MIX_HALF
        pairs = [(w_in_cols(_B1 + off), part(0)), (w_in_cols(_B2 + off), part(1)),
                 (w_in_cols(_B3 + off), part(2))]
    elif group in (3, 4):
        off = (group - 3) * MIX_HALF
        pairs = [(w_in_cols(_GA + off), part(0)), (w_in_cols(_GB + off), part(1)),
                 (wpa_hbm.at[layer, :, pl.ds(off, MIX_HALF)], part(2)),
                 (wpb_hbm.at[layer, :, pl.ds(off, MIX_HALF)], part(3))]
    else:
        pairs = [(wo_hbm.at[layer], ring.at[slot, :, pl.ds(0, D_MODEL)])]
    return [pltpu.make_async_copy(src, dst, sem.at[slot]) for src, dst in pairs]


def _mixer_kernel(h_hbm, x_hbm, w_in_hbm, wpa_hbm, wpb_hbm, wo_hbm, sgn_ref, sgw_ref, sgb_ref,
                  cw_ref, out_hbm, hbuf, obuf, ring, vt_ref, s_ref, tb_ref, g_ref, t_ref, v_ref,
                  wsem, hsem, xsem, osem, *, layer):
    i = pl.program_id(0)
    n_tiles = pl.num_programs(0)
    copies = functools.partial(_mixer_group_copies, w_in_hbm, wpa_hbm, wpb_hbm, wo_hbm, ring,
                               wsem, layer)

    def start(group):
        for cp in copies(group):
            cp.start()

    def wait(group):
        for cp in copies(group):
            cp.wait()

    def tile_rows(t):
        return pl.ds(pl.multiple_of(t * SEQ, SEQ), SEQ)

    x_copy = pltpu.make_async_copy(x_hbm.at[tile_rows(i)], obuf, xsem)
    out_copy = pltpu.make_async_copy(obuf, out_hbm.at[tile_rows(i)], osem)

    def h_copy(t):
        return pltpu.make_async_copy(h_hbm.at[tile_rows(t)], hbuf.at[t % 2], hsem)

    h_ref = hbuf.at[i % 2]

    def weight(slot, col, width=MIX_COLS):
        return ring[slot, :, col:col + width].astype(BF16)

    row_chunks = [slice(r * MIX_ROWS, (r + 1) * MIX_ROWS) for r in range(SEQ // MIX_ROWS)]

    @pl.when(i == 0)
    def _():
        h_copy(i).start()
        start(0)

    wait(0)
    h_copy(i).wait()
    start(1)
    wv_blocks = [weight(0, _V + n * MIX_COLS) for n in range(N_MIX_BLOCKS)]
    for r, rows in enumerate(row_chunks):
        for n in range(N_MIX_BLOCKS):
            v_ref[:, n * MIX_COLS:(n + 1) * MIX_COLS] = _gelu_tanh(_dot(h_ref[rows, :], wv_blocks[n]))
        vn = _rmsnorm(v_ref[...], sgn_ref[...]).astype(BF16)
        for c in range(MIX_ROWS // CHUNK):
            chunk = r * (MIX_ROWS // CHUNK) + c
            for head in range(HEADS):
                vt_ref[head * CHUNK:(head + 1) * CHUNK, chunk * CHUNK:(chunk + 1) * CHUNK] = (
                    vn[c * CHUNK:(c + 1) * CHUNK, head * HEAD_DIM:(head + 1) * HEAD_DIM])
    zero_block = jnp.zeros((CHUNK, CHUNK), BF16)
    for n in range(N_MIX_BLOCKS):
        h0, h1 = n * HEADS_PER_BLOCK, n * HEADS_PER_BLOCK + 1
        w_pair = jnp.concatenate([
            jnp.concatenate([sgw_ref[h0].astype(BF16), zero_block], axis=1),
            jnp.concatenate([zero_block, sgw_ref[h1].astype(BF16)], axis=1)], axis=0)
        bias = jnp.concatenate([sgb_ref[h0], sgb_ref[h1]], axis=0)
        g_ref[...] = _dot(w_pair, vt_ref[h0 * CHUNK:(h1 + 1) * CHUNK, :]) + bias
        wu = weight(0, n * MIX_COLS)
        for r, rows in enumerate(row_chunks):
            u = _gelu_tanh(_dot(h_ref[rows, :], wu))
            for hh in range(HEADS_PER_BLOCK):
                for c in range(MIX_ROWS // CHUNK):
                    chunk = r * (MIX_ROWS // CHUNK) + c
                    s_ref[chunk * CHUNK:(chunk + 1) * CHUNK,
                          (h0 + hh) * HEAD_DIM:(h0 + hh + 1) * HEAD_DIM] = (
                        u[c * CHUNK:(c + 1) * CHUNK, hh * HEAD_DIM:(hh + 1) * HEAD_DIM]
                        * g_ref[hh * CHUNK:(hh + 1) * CHUNK, chunk * CHUNK:(chunk + 1) * CHUNK]
                    ).astype(BF16)

    t_ref[0:CONV_PAD, :] = jnp.zeros((CONV_PAD, MIX_COLS), F32)
    t_ref[CONV_PAD + SEQ:, :] = jnp.zeros((CONV_PAD, MIX_COLS), F32)
    for group in (1, 2):
        wait(group)
        if group == 1:
            pl.when(i > 0)(lambda: pltpu.make_async_copy(
                obuf, out_hbm.at[tile_rows(i - 1)], osem).wait())
            x_copy.start()
            pl.when(i + 1 < n_tiles)(lambda: h_copy(i + 1).start())
        start(group + 1)
        slot = group % 2
        for q in range(MIX_HALF // MIX_COLS):
            n = (group - 1) * (MIX_HALF // MIX_COLS) + q
            col = q * MIX_COLS
            w_gate, w_c, w_x = (weight(slot, p * MIX_HALF + col) for p in range(3))
            for rows in row_chunks:
                h = h_ref[rows, :]
                t_ref[CONV_PAD + rows.start:CONV_PAD + rows.stop, :] = _dot(h, w_c) * _dot(h, w_x)
            for rows in row_chunks:
                lo = CONV_PAD + rows.start
                conv = (cw_ref[0, n] * t_ref[lo - 1:lo - 1 + MIX_ROWS, :]
                        + cw_ref[1, n] * t_ref[lo:lo + MIX_ROWS, :]
                        + cw_ref[2, n] * t_ref[lo + 1:lo + 1 + MIX_ROWS, :])
                tb_ref[rows, n * MIX_COLS:(n + 1) * MIX_COLS] = (
                    _dot(h_ref[rows, :], w_gate) * conv).astype(BF16)

    for group in (3, 4):
        wait(group)
        start(group + 1)
        slot = group % 2
        for q in range(MIX_HALF // MIX_COLS):
            n = (group - 3) * (MIX_HALF // MIX_COLS) + q
            col = q * MIX_COLS
            w_ga, w_gb, w_pa, w_pb = (weight(slot, p * MIX_HALF + col) for p in range(4))
            for r, rows in enumerate(row_chunks):
                h = h_ref[rows, :]
                merged = _sigmoid(_dot(h, w_ga)) * _dot(s_ref[rows, :], w_pa)
                merged = merged + _sigmoid(_dot(h, w_gb)) * _dot(tb_ref[rows, :], w_pb)
                half, local = divmod(rows.start, SEQ // 2)
                vt_ref[local:local + MIX_ROWS,
                       half * D_MODEL + n * MIX_COLS:half * D_MODEL + (n + 1) * MIX_COLS] = (
                    merged.astype(BF16))

    wait(5)
    x_copy.wait()
    pl.when(i + 1 < n_tiles)(lambda: start(0))
    for n in range(N_MIX_BLOCKS):
        cols = slice(n * MIX_COLS, (n + 1) * MIX_COLS)
        w = weight(1, n * MIX_COLS)
        for rows in row_chunks:
            half, local = divmod(rows.start, SEQ // 2)
            lhs = vt_ref[local:local + MIX_ROWS, half * D_MODEL:(half + 1) * D_MODEL]
            obuf[rows, cols] = obuf[rows, cols] + _dot(lhs, w)
    out_copy.start()
    pl.when(i == n_tiles - 1)(lambda: out_copy.wait())


def _mixer_call(h, x, w_in, sgu_norm, sgu_w, sgu_b, conv_w, w_proj_a, w_proj_b, w_out, layer):
    n_tok = x.shape[0]
    nt = n_tok // SEQ
    nb = N_MIX_BLOCKS
    hbm = pl.BlockSpec(memory_space=pl.ANY)
    in_specs = [
        hbm, hbm, hbm, hbm, hbm, hbm,
        pl.BlockSpec((None, 1, D_MODEL), lambda i: (layer, 0, 0)),
        pl.BlockSpec((None, HEADS, CHUNK, CHUNK), lambda i: (layer, 0, 0, 0)),
        pl.BlockSpec((None, HEADS, CHUNK, 1), lambda i: (layer, 0, 0, 0)),
        pl.BlockSpec((None, 3, nb, 1, MIX_COLS), lambda i: (layer, 0, 0, 0, 0)),
    ]
    return pl.pallas_call(
        functools.partial(_mixer_kernel, layer=layer),
        grid=(nt,),
        in_specs=in_specs,
        out_specs=hbm,
        out_shape=jax.ShapeDtypeStruct((n_tok, D_MODEL), F32),
        scratch_shapes=[
            pltpu.VMEM((2, SEQ, D_MODEL), BF16),
            pltpu.VMEM((SEQ, D_MODEL), F32),
            pltpu.VMEM((2, D_MODEL, MIX_RING_COLS), F32),
            pltpu.VMEM((HEADS * CHUNK, SEQ), BF16),
            pltpu.VMEM((SEQ, D_MODEL), BF16),
            pltpu.VMEM((SEQ, D_MODEL), BF16),
            pltpu.VMEM((HEADS_PER_BLOCK * CHUNK, SEQ), F32),
            pltpu.VMEM((SEQ + 2 * CONV_PAD, MIX_COLS), F32),
            pltpu.VMEM((MIX_ROWS, D_MODEL), F32),
            pltpu.SemaphoreType.DMA((2,)),
            pltpu.SemaphoreType.DMA,
            pltpu.SemaphoreType.DMA,
            pltpu.SemaphoreType.DMA,
        ],
        compiler_params=pltpu.CompilerParams(
            dimension_semantics=("arbitrary",),
            vmem_limit_bytes=VMEM_LIMIT_BYTES),
        name="mixer",
    )(h, x, w_in, w_proj_a, w_proj_b, w_out, sgu_norm, sgu_w, sgu_b, conv_w)


def kernel(x, ffn1_norm, ffn1_w_in, ffn1_w_out, mix_norm, w_in, sgu_norm, sgu_w, sgu_b, conv_w,
           w_proj_a, w_proj_b, w_out, ffn2_norm, ffn2_w_in, ffn2_w_out, final_norm):
    batch, seq, d = x.shape
    depth = ffn1_norm.shape[0]
    assert (seq, d) == (SEQ, D_MODEL) and w_in.shape[-1] == N_IN and ffn1_w_out.shape[1] == D_FF
    xt = x.reshape(batch * seq, d)
    gains = lambda g: g.reshape(depth, 1, d)
    ffn1_g, mix_g, ffn2_g, sgu_g = gains(ffn1_norm), gains(mix_norm), gains(ffn2_norm), gains(sgu_norm)
    final_g = final_norm.reshape(1, 1, d)
    sgu_b4 = sgu_b.reshape(depth, HEADS, CHUNK, 1)
    conv_w5 = conv_w.reshape(depth, 3, N_MIX_BLOCKS, 1, MIX_COLS)
    for l in range(depth):
        xt, h = _ffn_call(xt, ffn1_g, ffn1_w_in, ffn1_w_out, mix_g, l, l, "emit_h")
        xt = _mixer_call(h, xt, w_in, sgu_g, sgu_w, sgu_b4, conv_w5, w_proj_a, w_proj_b, w_out, l)
        if l == depth - 1:
            (xt,) = _ffn_call(xt, ffn2_g, ffn2_w_in, ffn2_w_out, final_g, l, 0, "final")
        else:
            (xt,) = _ffn_call(xt, ffn2_g, ffn2_w_in, ffn2_w_out, ffn2_g, l, l, "plain")
    return xt.reshape(batch, seq, d)
```

```python
import functools

import numpy as np
import jax
import jax.numpy as jnp
from jax import lax
from jax.experimental import pallas as pl
from jax.experimental.pallas import tpu as pltpu

F32 = jnp.float32
BF16 = jnp.bfloat16
EPS = 1e-6

D_MODEL = 1024
SEQ = 2048
CHUNK = 128
HEADS = 8
HEAD_DIM = 128
D_FF = 2816
N_IN = 7 * D_MODEL

TOKENS_PER_TILE = SEQ
FFN_COLS = 256
FFN_ROWS = 1024
N_FFN_CHUNKS = D_FF // FFN_COLS
FFN_GROUP = 4
N_FFN_GROUPS = -(-N_FFN_CHUNKS // FFN_GROUP)
FFN_SLOTS = 2
MIX_COLS = 256
MIX_ROWS = 1024
CONV_PAD = 8
N_MIX_BLOCKS = D_MODEL // MIX_COLS
N_CHUNKS = SEQ // CHUNK
HEADS_PER_BLOCK = MIX_COLS // HEAD_DIM
VMEM_LIMIT_BYTES = 60 * 1024 * 1024


def _rmsnorm(x, g):
    ms = jnp.mean(x * x, axis=-1, keepdims=True)
    return (x * lax.rsqrt(ms + EPS)) * g


def _gelu_tanh(x):
    c = np.float32(np.sqrt(2.0 / np.pi))
    c1 = np.float32(np.sqrt(2.0 / np.pi) * 0.044715)
    half_x = 0.5 * x
    return half_x + half_x * jnp.tanh(x * (c + c1 * (x * x)))


def _sigmoid(x):
    return 0.5 * jnp.tanh(0.5 * x) + 0.5


def _dot(a, b):
    return jnp.dot(a, b, preferred_element_type=F32)


def _ffn_group_chunks(group):
    return range(group * FFN_GROUP, min((group + 1) * FFN_GROUP, N_FFN_CHUNKS))


def _ffn_group_copies(w_in_hbm, w_out_hbm, wg_buf, wu_buf, wo_buf, sem, layer, slot, group):
    copies = []
    for k, chunk in enumerate(_ffn_group_chunks(group)):
        cols = pl.ds(chunk * FFN_COLS, FFN_COLS)
        up_cols = pl.ds(D_FF + chunk * FFN_COLS, FFN_COLS)
        copies += [
            pltpu.make_async_copy(w_in_hbm.at[layer, :, cols], wg_buf.at[slot, k], sem.at[slot]),
            pltpu.make_async_copy(w_in_hbm.at[layer, :, up_cols], wu_buf.at[slot, k], sem.at[slot]),
            pltpu.make_async_copy(w_out_hbm.at[layer, cols, :], wo_buf.at[slot, k], sem.at[slot]),
        ]
    return copies


def _ffn_kernel(g_ref, gb_ref, g2_ref, x_hbm, w_in_hbm, w_out_hbm, w_in_b_hbm, w_out_b_hbm, out_hbm,
                *refs, mode, layers):
    if mode == "emit_h":
        hout_ref, obuf, h_ref, act_ref, wg_buf, wu_buf, wo_buf, sem, xsem, osem = refs
    else:
        obuf, h_ref, act_ref, wg_buf, wu_buf, wo_buf, sem, xsem, osem = refs
    i = pl.program_id(0)
    n_tiles = pl.num_programs(0)
    stage_weights = ((w_in_hbm, w_out_hbm), (w_in_b_hbm, w_out_b_hbm))
    stage_gains = (g_ref, gb_ref)
    n_groups = N_FFN_GROUPS * len(layers)

    def copies(slot, group):
        stage = group // N_FFN_GROUPS
        w_in_s, w_out_s = stage_weights[stage]
        return _ffn_group_copies(w_in_s, w_out_s, wg_buf, wu_buf, wo_buf, sem, layers[stage],
                                 slot, group % N_FFN_GROUPS)

    n_row_chunks = TOKENS_PER_TILE // FFN_ROWS
    tile = obuf.at[i % 2]

    def ring_slot(t, group):
        return (t * n_groups + group) % FFN_SLOTS

    def start(t, group):
        for cp in copies(ring_slot(t, group), group):
            cp.start()

    def wait(t, group):
        for cp in copies(ring_slot(t, group), group):
            cp.wait()

    def tile_rows(t):
        return pl.ds(pl.multiple_of(t * TOKENS_PER_TILE, TOKENS_PER_TILE), TOKENS_PER_TILE)

    def x_copy(t):
        return pltpu.make_async_copy(x_hbm.at[tile_rows(t)], obuf.at[t % 2], xsem)

    def out_copy(t):
        return pltpu.make_async_copy(obuf.at[t % 2], out_hbm.at[tile_rows(t)], osem)

    @pl.when(i == 0)
    def _():
        x_copy(i).start()
        start(i, 0)

    for group in range(n_groups):
        g = group % N_FFN_GROUPS
        wait(i, group)
        if group == 0:
            x_copy(i).wait()
        if group == 1:
            pl.when(i > 0)(lambda: out_copy(i - 1).wait())
            pl.when(i + 1 < n_tiles)(lambda: x_copy(i + 1).start())
        if group + 1 < n_groups:
            start(i, group + 1)
        else:
            pl.when(i + 1 < n_tiles)(lambda: start(i + 1, 0))
        if g == 0:
            gain = stage_gains[group // N_FFN_GROUPS]
            for r in range(n_row_chunks):
                rows = pl.ds(r * FFN_ROWS, FFN_ROWS)
                h_ref[rows, :] = _rmsnorm(tile[rows, :], gain[...]).astype(BF16)
        slot = ring_slot(i, group)
        n_chunks = len(_ffn_group_chunks(g))
        for k in range(n_chunks):
            wg = wg_buf[slot, k].astype(BF16)
            wu = wu_buf[slot, k].astype(BF16)
            for r in range(n_row_chunks):
                rows = pl.ds(r * FFN_ROWS, FFN_ROWS)
                h = h_ref[rows, :]
                gate = _dot(h, wg)
                up = _dot(h, wu)
                act_ref[rows, k * FFN_COLS:(k + 1) * FFN_COLS] = (
                    0.5 * (gate * _sigmoid(gate)) * up).astype(BF16)
        wo = wo_buf[slot, 0:n_chunks].reshape(n_chunks * FFN_COLS, D_MODEL).astype(BF16)
        for r in range(n_row_chunks):
            rows = pl.ds(r * FFN_ROWS, FFN_ROWS)
            tile[rows, :] = tile[rows, :] + _dot(act_ref[rows, 0:n_chunks * FFN_COLS], wo)

    if mode != "plain":
        for r in range(n_row_chunks):
            rows = pl.ds(r * FFN_ROWS, FFN_ROWS)
            y = _rmsnorm(tile[rows, :], g2_ref[...])
            if mode == "emit_h":
                hout_ref[rows, :] = y.astype(BF16)
            else:
                tile[rows, :] = y
    out_copy(i).start()
    pl.when(i == n_tiles - 1)(lambda: out_copy(i).wait())


def _ffn_call(x, stages, g2, g2_layer, mode):
    n_tok = x.shape[0]
    nt = n_tok // TOKENS_PER_TILE
    tm = TOKENS_PER_TILE
    hbm = pl.BlockSpec(memory_space=pl.ANY)
    (norm, w_in, w_out, layer), (norm_b, w_in_b, w_out_b, layer_b) = stages[0], stages[-1]
    layers = tuple(s[3] for s in stages)
    in_specs = [
        pl.BlockSpec((None, 1, D_MODEL), lambda i: (layer, 0, 0)),
        pl.BlockSpec((None, 1, D_MODEL), lambda i: (layer_b, 0, 0)),
        pl.BlockSpec((None, 1, D_MODEL), lambda i: (g2_layer, 0, 0)),
        hbm, hbm, hbm, hbm, hbm,
    ]
    out_shape = [jax.ShapeDtypeStruct((n_tok, D_MODEL), F32)]
    out_specs = [hbm]
    if mode == "emit_h":
        out_shape.append(jax.ShapeDtypeStruct((n_tok, D_MODEL), BF16))
        out_specs.append(pl.BlockSpec((tm, D_MODEL), lambda i: (i, 0)))
    return pl.pallas_call(
        functools.partial(_ffn_kernel, mode=mode, layers=layers),
        grid=(nt,),
        in_specs=in_specs,
        out_specs=out_specs,
        out_shape=out_shape,
        scratch_shapes=[
            pltpu.VMEM((2, tm, D_MODEL), F32),
            pltpu.VMEM((tm, D_MODEL), BF16),
            pltpu.VMEM((tm, FFN_GROUP * FFN_COLS), BF16),
            pltpu.VMEM((FFN_SLOTS, FFN_GROUP, D_MODEL, FFN_COLS), F32),
            pltpu.VMEM((FFN_SLOTS, FFN_GROUP, D_MODEL, FFN_COLS), F32),
            pltpu.VMEM((FFN_SLOTS, FFN_GROUP, FFN_COLS, D_MODEL), F32),
            pltpu.SemaphoreType.DMA((FFN_SLOTS,)),
            pltpu.SemaphoreType.DMA,
            pltpu.SemaphoreType.DMA,
        ],
        compiler_params=pltpu.CompilerParams(
            dimension_semantics=("arbitrary",),
            vmem_limit_bytes=VMEM_LIMIT_BYTES),
        name=f"ffn_{mode}" + ("_pair" if len(stages) > 1 else ""),
    )(norm, norm_b, g2, x, w_in, w_out, w_in_b, w_out_b)


MIX_HALF = D_MODEL // 2
MIX_RING_COLS = 2 * D_MODEL
N_MIX_GROUPS = 6
_U, _V, _B1, _B2, _B3, _GA, _GB = (k * D_MODEL for k in range(7))


def _mixer_group_copies(w_in_hbm, wpa_hbm, wpb_hbm, wo_hbm, ring, sem, layer, group):
    slot = group % 2

    def part(p):
        return ring.at[slot, :, pl.ds(p * MIX_HALF, MIX_HALF)]

    def w_in_cols(start):
        return w_in_hbm.at[layer, :, pl.ds(start, MIX_HALF)]

    if group == 0:
        pairs = [(w_in_hbm.at[layer, :, pl.ds(_U, 2 * D_MODEL)], ring.at[slot])]
    elif group in (1, 2):
        off = (group - 1) * MIX_HALF
        pairs = [(w_in_cols(_B1 + off), part(0)), (w_in_cols(_B2 + off), part(1)),
                 (w_in_cols(_B3 + off), part(2))]
    elif group in (3, 4):
        off = (group - 3) * MIX_HALF
        pairs = [(w_in_cols(_GA + off), part(0)), (w_in_cols(_GB + off), part(1)),
                 (wpa_hbm.at[layer, :, pl.ds(off, MIX_HALF)], part(2)),
                 (wpb_hbm.at[layer, :, pl.ds(off, MIX_HALF)], part(3))]
    else:
        pairs = [(wo_hbm.at[layer], ring.at[slot, :, pl.ds(0, D_MODEL)])]
    return [pltpu.make_async_copy(src, dst, sem.at[slot]) for src, dst in pairs]


def _mixer_kernel(h_ref, x_hbm, w_in_hbm, wpa_hbm, wpb_hbm, wo_hbm, sgn_ref, sgw_ref, sgb_ref,
                  cw_ref, out_hbm, obuf, ring, vt_ref, s_ref, tb_ref, g_ref, t_ref, v_ref,
                  wsem, xsem, osem, *, layer):
    i = pl.program_id(0)
    n_tiles = pl.num_programs(0)
    copies = functools.partial(_mixer_group_copies, w_in_hbm, wpa_hbm, wpb_hbm, wo_hbm, ring,
                               wsem, layer)

    def start(group):
        for cp in copies(group):
            cp.start()

    def wait(group):
        for cp in copies(group):
            cp.wait()

    def tile_rows(t):
        return pl.ds(pl.multiple_of(t * SEQ, SEQ), SEQ)

    x_copy = pltpu.make_async_copy(x_hbm.at[tile_rows(i)], obuf, xsem)
    out_copy = pltpu.make_async_copy(obuf, out_hbm.at[tile_rows(i)], osem)

    def weight(slot, col, width=MIX_COLS):
        return ring[slot, :, col:col + width].astype(BF16)

    row_chunks = [slice(r * MIX_ROWS, (r + 1) * MIX_ROWS) for r in range(SEQ // MIX_ROWS)]

    pl.when(i == 0)(lambda: start(0))

    wait(0)
    start(1)
    wv_blocks = [weight(0, _V + n * MIX_COLS) for n in range(N_MIX_BLOCKS)]
    for r, rows in enumerate(row_chunks):
        for n in range(N_MIX_BLOCKS):
            v_ref[:, n * MIX_COLS:(n + 1) * MIX_COLS] = _gelu_tanh(_dot(h_ref[rows, :], wv_blocks[n]))
        vn = _rmsnorm(v_ref[...], sgn_ref[...]).astype(BF16)
        for c in range(MIX_ROWS // CHUNK):
            chunk = r * (MIX_ROWS // CHUNK) + c
            for head in range(HEADS):
                vt_ref[head * CHUNK:(head + 1) * CHUNK, chunk * CHUNK:(chunk + 1) * CHUNK] = (
                    vn[c * CHUNK:(c + 1) * CHUNK, head * HEAD_DIM:(head + 1) * HEAD_DIM])
    zero_block = jnp.zeros((CHUNK, CHUNK), BF16)
    for n in range(N_MIX_BLOCKS):
        h0, h1 = n * HEADS_PER_BLOCK, n * HEADS_PER_BLOCK + 1
        w_pair = jnp.concatenate([
            jnp.concatenate([sgw_ref[h0].astype(BF16), zero_block], axis=1),
            jnp.concatenate([zero_block, sgw_ref[h1].astype(BF16)], axis=1)], axis=0)
        bias = jnp.concatenate([sgb_ref[h0], sgb_ref[h1]], axis=0)
        g_ref[...] = _dot(w_pair, vt_ref[h0 * CHUNK:(h1 + 1) * CHUNK, :]) + bias
        wu = weight(0, n * MIX_COLS)
        for r, rows in enumerate(row_chunks):
            u = _gelu_tanh(_dot(h_ref[rows, :], wu))
            for hh in range(HEADS_PER_BLOCK):
                for c in range(MIX_ROWS // CHUNK):
                    chunk = r * (MIX_ROWS // CHUNK) + c
                    s_ref[chunk * CHUNK:(chunk + 1) * CHUNK,
                          (h0 + hh) * HEAD_DIM:(h0 + hh + 1) * HEAD_DIM] = (
                        u[c * CHUNK:(c + 1) * CHUNK, hh * HEAD_DIM:(hh + 1) * HEAD_DIM]
                        * g_ref[hh * CHUNK:(hh + 1) * CHUNK, chunk * CHUNK:(chunk + 1) * CHUNK]
                    ).astype(BF16)

    t_ref[0:CONV_PAD, :] = jnp.zeros((CONV_PAD, MIX_COLS), F32)
    t_ref[CONV_PAD + SEQ:, :] = jnp.zeros((CONV_PAD, MIX_COLS), F32)
    for group in (1, 2):
        wait(group)
        if group == 1:
            pl.when(i > 0)(lambda: pltpu.make_async_copy(
                obuf, out_hbm.at[tile_rows(i - 1)], osem).wait())
            x_copy.start()
        start(group + 1)
        slot = group % 2
        for q in range(MIX_HALF // MIX_COLS):
            n = (group - 1) * (MIX_HALF // MIX_COLS) + q
            col = q * MIX_COLS
            w_gate, w_c, w_x = (weight(slot, p * MIX_HALF + col) for p in range(3))
            for rows in row_chunks:
                h = h_ref[rows, :]
                t_ref[CONV_PAD + rows.start:CONV_PAD + rows.stop, :] = _dot(h, w_c) * _dot(h, w_x)
            for rows in row_chunks:
                lo = CONV_PAD + rows.start
                conv = (cw_ref[0, n] * t_ref[lo - 1:lo - 1 + MIX_ROWS, :]
                        + cw_ref[1, n] * t_ref[lo:lo + MIX_ROWS, :]
                        + cw_ref[2, n] * t_ref[lo + 1:lo + 1 + MIX_ROWS, :])
                tb_ref[rows, n * MIX_COLS:(n + 1) * MIX_COLS] = (
                    _dot(h_ref[rows, :], w_gate) * conv).astype(BF16)

    for group in (3, 4):
        wait(group)
        start(group + 1)
        slot = group % 2
        for q in range(MIX_HALF // MIX_COLS):
            n = (group - 3) * (MIX_HALF // MIX_COLS) + q
            col = q * MIX_COLS
            w_ga, w_gb, w_pa, w_pb = (weight(slot, p * MIX_HALF + col) for p in range(4))
            for r, rows in enumerate(row_chunks):
                h = h_ref[rows, :]
                merged = _sigmoid(_dot(h, w_ga)) * _dot(s_ref[rows, :], w_pa)
                merged = merged + _sigmoid(_dot(h, w_gb)) * _dot(tb_ref[rows, :], w_pb)
                half, local = divmod(rows.start, SEQ // 2)
                vt_ref[local:local + MIX_ROWS,
                       half * D_MODEL + n * MIX_COLS:half * D_MODEL + (n + 1) * MIX_COLS] = (
                    merged.astype(BF16))

    wait(5)
    x_copy.wait()
    pl.when(i + 1 < n_tiles)(lambda: start(0))
    for n in range(N_MIX_BLOCKS):
        cols = slice(n * MIX_COLS, (n + 1) * MIX_COLS)
        w = weight(1, n * MIX_COLS)
        for rows in row_chunks:
            half, local = divmod(rows.start, SEQ // 2)
            lhs = vt_ref[local:local + MIX_ROWS, half * D_MODEL:(half + 1) * D_MODEL]
            obuf[rows, cols] = obuf[rows, cols] + _dot(lhs, w)
    out_copy.start()
    pl.when(i == n_tiles - 1)(lambda: out_copy.wait())


def _mixer_call(h, x, w_in, sgu_norm, sgu_w, sgu_b, conv_w, w_proj_a, w_proj_b, w_out, layer):
    n_tok = x.shape[0]
    nt = n_tok // SEQ
    nb = N_MIX_BLOCKS
    hbm = pl.BlockSpec(memory_space=pl.ANY)
    in_specs = [
        pl.BlockSpec((SEQ, D_MODEL), lambda i: (i, 0)),
        hbm, hbm, hbm, hbm, hbm,
        pl.BlockSpec((None, 1, D_MODEL), lambda i: (layer, 0, 0)),
        pl.BlockSpec((None, HEADS, CHUNK, CHUNK), lambda i: (layer, 0, 0, 0)),
        pl.BlockSpec((None, HEADS, CHUNK, 1), lambda i: (layer, 0, 0, 0)),
        pl.BlockSpec((None, 3, nb, 1, MIX_COLS), lambda i: (layer, 0, 0, 0, 0)),
    ]
    return pl.pallas_call(
        functools.partial(_mixer_kernel, layer=layer),
        grid=(nt,),
        in_specs=in_specs,
        out_specs=hbm,
        out_shape=jax.ShapeDtypeStruct((n_tok, D_MODEL), F32),
        scratch_shapes=[
            pltpu.VMEM((SEQ, D_MODEL), F32),
            pltpu.VMEM((2, D_MODEL, MIX_RING_COLS), F32),
            pltpu.VMEM((HEADS * CHUNK, SEQ), BF16),
            pltpu.VMEM((SEQ, D_MODEL), BF16),
            pltpu.VMEM((SEQ, D_MODEL), BF16),
            pltpu.VMEM((HEADS_PER_BLOCK * CHUNK, SEQ), F32),
            pltpu.VMEM((SEQ + 2 * CONV_PAD, MIX_COLS), F32),
            pltpu.VMEM((MIX_ROWS, D_MODEL), F32),
            pltpu.SemaphoreType.DMA((2,)),
            pltpu.SemaphoreType.DMA,
            pltpu.SemaphoreType.DMA,
        ],
        compiler_params=pltpu.CompilerParams(
            dimension_semantics=("arbitrary",),
            vmem_limit_bytes=VMEM_LIMIT_BYTES),
        name="mixer",
    )(h, x, w_in, w_proj_a, w_proj_b, w_out, sgu_norm, sgu_w, sgu_b, conv_w)


def kernel(x, ffn1_norm, ffn1_w_in, ffn1_w_out, mix_norm, w_in, sgu_norm, sgu_w, sgu_b, conv_w,
           w_proj_a, w_proj_b, w_out, ffn2_norm, ffn2_w_in, ffn2_w_out, final_norm):
    batch, seq, d = x.shape
    depth = ffn1_norm.shape[0]
    assert (seq, d) == (SEQ, D_MODEL) and w_in.shape[-1] == N_IN and ffn1_w_out.shape[1] == D_FF
    xt = x.reshape(batch * seq, d)
    gains = lambda g: g.reshape(depth, 1, d)
    ffn1_g, mix_g, ffn2_g, sgu_g = gains(ffn1_norm), gains(mix_norm), gains(ffn2_norm), gains(sgu_norm)
    final_g = final_norm.reshape(1, 1, d)
    sgu_b4 = sgu_b.reshape(depth, HEADS, CHUNK, 1)
    conv_w5 = conv_w.reshape(depth, 3, N_MIX_BLOCKS, 1, MIX_COLS)
    ffn1 = lambda l: (ffn1_g, ffn1_w_in, ffn1_w_out, l)
    ffn2 = lambda l: (ffn2_g, ffn2_w_in, ffn2_w_out, l)
    xt, h = _ffn_call(xt, [ffn1(0)], mix_g, 0, "emit_h")
    for l in range(depth):
        xt = _mixer_call(h, xt, w_in, sgu_g, sgu_w, sgu_b4, conv_w5, w_proj_a, w_proj_b, w_out, l)
        if l == depth - 1:
            (xt,) = _ffn_call(xt, [ffn2(l)], final_g, 0, "final")
        else:
            xt, h = _ffn_call(xt, [ffn2(l), ffn1(l + 1)], mix_g, l + 1, "emit_h")
    return xt.reshape(batch, seq, d)
```
